```python
import jax, jax.numpy as jnp
from jax import lax
import numpy as np

D_MODEL = 2048
BATCH = 2
SEQ = 16384
DEPTH = 4

POOL_WIDTH = D_MODEL // 2
POOL_WINDOWS = (2, 4, 8, 16)
N_POOL_GROUPS = len(POOL_WINDOWS)
POOL_GROUP_DIM = POOL_WIDTH // N_POOL_GROUPS
MLSTM_WIDTH = D_MODEL
MLSTM_HEADS = 8
MLSTM_HEAD_DIM = MLSTM_WIDTH // MLSTM_HEADS
CHUNK = 128
N_EXPERTS = 16
EC_CAPACITY_FACTOR = 2
EXPERT_FF = D_MODEL // 2
N_GATE_COLS = 4 * MLSTM_HEADS
COL_POOL = 0
COL_Q = COL_POOL + POOL_WIDTH
COL_K = COL_Q + MLSTM_WIDTH
COL_V = COL_K + MLSTM_WIDTH
COL_O = COL_V + MLSTM_WIDTH
COL_IF = COL_O + MLSTM_WIDTH
COL_GP = COL_IF + N_GATE_COLS
COL_GM = COL_GP + D_MODEL
N_IN = COL_GM + D_MODEL
DEEPNORM_ALPHA = (2.0 * DEPTH) ** 0.25
DEEPNORM_BETA = (8.0 * DEPTH) ** -0.25
LN_EPS = 1e-5
GN_EPS = 1e-6

kernel_name = "pool_mlstm_ec_moe_deepnorm_encoder"


def layer_norm(x, g, b):
    xf = x.astype(jnp.float32)
    mu = jnp.mean(xf, axis=-1, keepdims=True)
    var = jnp.mean(jnp.square(xf - mu), axis=-1, keepdims=True)
    return ((xf - mu) * lax.rsqrt(var + LN_EPS) * g.astype(jnp.float32) + b.astype(jnp.float32)).astype(x.dtype)


def pool_mixer(u, w_grp, scale):
    bsz, seq, _ = u.shape
    t = jnp.arange(seq)
    ug = u.astype(jnp.float32).reshape(bsz, seq, N_POOL_GROUPS, POOL_GROUP_DIM)
    outs = []
    for grp, win in enumerate(POOL_WINDOWS):
        xg = ug[:, :, grp]
        cs = jnp.concatenate([jnp.zeros((bsz, 1, POOL_GROUP_DIM), jnp.float32),
                              jnp.cumsum(xg, axis=1)], axis=1)
        lo = jnp.clip(t - win // 2, 0, seq)
        hi = jnp.clip(t + win // 2, 0, seq)
        cnt = (hi - lo).astype(jnp.float32)[None, :, None]
        diff = (cs[:, hi] - cs[:, lo]) / cnt - xg
        outs.append(jnp.einsum('bsc,ce->bse', diff.astype(u.dtype), w_grp[grp]))
    return jnp.concatenate(outs, axis=-1) * scale


def mlstm_scan(q, k, v, log_i, log_f):
    bsz, nh, seq, dh = q.shape
    nc = seq // CHUNK

    def chunks(a):
        a = a.reshape((bsz, nh, nc, CHUNK) + a.shape[3:])
        return jnp.moveaxis(a, 2, 0)

    lower_tri = jnp.tril(jnp.ones((CHUNK, CHUNK), dtype=bool))

    def step(carry, xs):
        c_st, n_st, m_st = carry
        qc, kc, vc, ic, fc = xs
        b = jnp.cumsum(fc, axis=-1)
        g = b[..., -1]
        dmat = b[..., :, None] - b[..., None, :] + ic[..., None, :]
        dmat = jnp.where(lower_tri, dmat, -jnp.inf)
        inter = b + m_st[..., None]
        m_t = jnp.maximum(inter, jnp.max(dmat, axis=-1))
        a_mat = jnp.einsum('bhtk,bhsk->bhts', qc, kc) * jnp.exp(dmat - m_t[..., None])
        w_inter = jnp.exp(inter - m_t)
        num = (jnp.einsum('bhts,bhsv->bhtv', a_mat, vc)
               + w_inter[..., None] * jnp.einsum('bhvk,bhtk->bhtv', c_st, qc))
        den = jnp.sum(a_mat, axis=-1) + w_inter * jnp.einsum('bhk,bhtk->bht', n_st, qc)
        h = num / jnp.maximum(jnp.abs(den), jnp.exp(-m_t))[..., None]
        a_log = g[..., None] - b + ic
        m_new = jnp.maximum(g + m_st, jnp.max(a_log, axis=-1))
        decay = jnp.exp(g + m_st - m_new)
        wa = jnp.exp(a_log - m_new[..., None])
        c_new = decay[..., None, None] * c_st + jnp.einsum('bhsv,bhsk->bhvk', vc * wa[..., None], kc)
        n_new = decay[..., None] * n_st + jnp.einsum('bhs,bhsk->bhk', wa, kc)
        return (c_new, n_new, m_new), h

    carry0 = (jnp.zeros((bsz, nh, dh, dh), jnp.float32),
              jnp.zeros((bsz, nh, dh), jnp.float32),
              jnp.zeros((bsz, nh), jnp.float32))
    _, hs = lax.scan(step, carry0, (chunks(q), chunks(k), chunks(v), chunks(log_i), chunks(log_f)))
    return jnp.moveaxis(hs, 0, 2).reshape(bsz, nh, seq, dh)


def mlstm_mixer(q, k, v, o, gate_pre, b_if, norm_w):
    bsz, seq, _ = q.shape

    def heads(a):
        return a.astype(jnp.float32).reshape(bsz, seq, MLSTM_HEADS, MLSTM_HEAD_DIM).transpose(0, 2, 1, 3)

    qh = heads(q) * (MLSTM_HEAD_DIM ** -0.5)
    kh = heads(k)
    vh = heads(v)
    gts = (gate_pre.astype(jnp.float32) + b_if.astype(jnp.float32)).transpose(0, 2, 1)
    i_f, f_f, i_b, f_b = jnp.split(gts, 4, axis=1)
    h_fwd = mlstm_scan(qh, kh, vh, i_f, jax.nn.log_sigmoid(f_f))
    flip = lambda a: jnp.flip(a, axis=2)
    h_bwd = flip(mlstm_scan(flip(qh), flip(kh), flip(vh), flip(i_b), flip(jax.nn.log_sigmoid(f_b))))
    h = h_fwd + h_bwd
    mu = jnp.mean(h, axis=-1, keepdims=True)
    var = jnp.mean(jnp.square(h - mu), axis=-1, keepdims=True)
    hn = ((h - mu) * lax.rsqrt(var + GN_EPS)).transpose(0, 2, 1, 3).reshape(bsz, seq, MLSTM_WIDTH)
    hn = hn * norm_w.astype(jnp.float32)
    return (jax.nn.sigmoid(o.astype(jnp.float32)) * hn).astype(q.dtype)


def expert_choice_moe(x, w_router, w_gate, w_up, w_down):
    bsz, seq, dm = x.shape
    cap = EC_CAPACITY_FACTOR * seq // N_EXPERTS
    aff = jax.nn.softmax((x @ w_router).astype(jnp.float32), axis=-1)
    gate, idx = lax.top_k(aff.transpose(0, 2, 1), cap)
    xe = jax.vmap(lambda xb, ib: xb[ib])(x, idx)
    hid = (jax.nn.silu(jnp.einsum('becd,edf->becf', xe, w_gate))
           * jnp.einsum('becd,edf->becf', xe, w_up))
    y = jnp.einsum('becf,efd->becd', hid, w_down) * gate[..., None].astype(x.dtype)
    return jax.vmap(lambda ib, yb: jnp.zeros((seq, dm), y.dtype)
                    .at[ib.reshape(-1)].add(yb.reshape(-1, dm)))(idx, y)


def setup_inputs(seed: int = 0) -> dict:
    key = jax.random.key(seed)
    ks = jax.random.split(key, 20)
    f32 = jnp.float32
    nrm = lambda k, shape, s: jax.random.normal(k, shape, f32) * s
    x = jax.random.normal(ks[0], (BATCH, SEQ, D_MODEL), f32)
    w_in = nrm(ks[1], (DEPTH, D_MODEL, N_IN), D_MODEL ** -0.5)
    i_bias = nrm(ks[2], (DEPTH, MLSTM_HEADS), 0.1)
    f_bias = jnp.linspace(3.0, 6.0, MLSTM_HEADS, dtype=f32)[None, :] + nrm(ks[3], (DEPTH, MLSTM_HEADS), 0.1)
    i_bias_b = nrm(ks[4], (DEPTH, MLSTM_HEADS), 0.1)
    f_bias_b = jnp.linspace(3.0, 6.0, MLSTM_HEADS, dtype=f32)[None, :] + nrm(ks[5], (DEPTH, MLSTM_HEADS), 0.1)
    b_if = jnp.concatenate([i_bias, f_bias, i_bias_b, f_bias_b], axis=-1)
    w_pool_grp = nrm(ks[6], (DEPTH, N_POOL_GROUPS, POOL_GROUP_DIM, POOL_GROUP_DIM), POOL_GROUP_DIM ** -0.5)
    pool_scale = 1.0 + nrm(ks[7], (DEPTH, POOL_WIDTH), 0.1)
    w_pool_up = nrm(ks[8], (DEPTH, POOL_WIDTH, D_MODEL), POOL_WIDTH ** -0.5)
    mlstm_norm_w = 1.0 + nrm(ks[9], (DEPTH, MLSTM_WIDTH), 0.02)
    w_mlstm_up = nrm(ks[10], (DEPTH, MLSTM_WIDTH, D_MODEL), MLSTM_WIDTH ** -0.5)
    w_out = nrm(ks[11], (DEPTH, D_MODEL, D_MODEL), D_MODEL ** -0.5 * DEEPNORM_BETA)
    ln1_g = 1.0 + nrm(ks[12], (DEPTH, D_MODEL), 0.02)
    ln1_b = nrm(ks[13], (DEPTH, D_MODEL), 0.02)
    w_router = nrm(ks[14], (DEPTH, D_MODEL, N_EXPERTS), D_MODEL ** -0.5)
    w_gate = nrm(ks[15], (DEPTH, N_EXPERTS, D_MODEL, EXPERT_FF), D_MODEL ** -0.5)
    w_up = nrm(ks[16], (DEPTH, N_EXPERTS, D_MODEL, EXPERT_FF), D_MODEL ** -0.5)
    w_down = nrm(ks[17], (DEPTH, N_EXPERTS, EXPERT_FF, D_MODEL), EXPERT_FF ** -0.5 * DEEPNORM_BETA)
    ln2_g = 1.0 + nrm(ks[18], (DEPTH, D_MODEL), 0.02)
    ln2_b = nrm(ks[19], (DEPTH, D_MODEL), 0.02)
    return {"x": x, "w_in": w_in, "b_if": b_if, "w_pool_grp": w_pool_grp, "pool_scale": pool_scale,
            "w_pool_up": w_pool_up, "mlstm_norm_w": mlstm_norm_w, "w_mlstm_up": w_mlstm_up,
            "w_out": w_out, "ln1_g": ln1_g, "ln1_b": ln1_b, "w_router": w_router,
            "w_gate": w_gate, "w_up": w_up, "w_down": w_down, "ln2_g": ln2_g, "ln2_b": ln2_b}


def reference(x, w_in, b_if, w_pool_grp, pool_scale, w_pool_up, mlstm_norm_w, w_mlstm_up,
              w_out, ln1_g, ln1_b, w_router, w_gate, w_up, w_down, ln2_g, ln2_b):
    for layer in range(DEPTH):
        wl = w_in[layer]
        proj = lambda lo, hi: x @ wl[:, lo:hi]
        pool_in = proj(COL_POOL, COL_Q)
        q = proj(COL_Q, COL_K)
        k = proj(COL_K, COL_V)
        v = proj(COL_V, COL_O)
        o = proj(COL_O, COL_IF)
        gate_pre = proj(COL_IF, COL_GP)
        gp = jax.nn.sigmoid(proj(COL_GP, COL_GM))
        gm = jax.nn.sigmoid(proj(COL_GM, N_IN))
        y_pool = pool_mixer(pool_in, w_pool_grp[layer], pool_scale[layer]) @ w_pool_up[layer]
        y_mlstm = mlstm_mixer(q, k, v, o, gate_pre, b_if[layer], mlstm_norm_w[layer]) @ w_mlstm_up[layer]
        mix = (gp * y_pool + gm * y_mlstm) @ w_out[layer]
        x = layer_norm(DEEPNORM_ALPHA * x + mix, ln1_g[layer], ln1_b[layer])
        ffn = expert_choice_moe(x, w_router[layer], w_gate[layer], w_up[layer], w_down[layer])
        x = layer_norm(DEEPNORM_ALPHA * x + ffn, ln2_g[layer], ln2_b[layer])
    return x
```

```python
import functools

import jax
import jax.numpy as jnp
from jax import lax
from jax.experimental import pallas as pl
from jax.experimental.pallas import tpu as pltpu

MLSTM_HEADS = 8
POOL_WINDOWS = (2, 4, 8, 16)
POOL_HALO = max(POOL_WINDOWS) // 2
EC_CAPACITY_FACTOR = 2
LN_EPS = 1e-5
GN_EPS = 1e-6

LANES = 128
SUBLANES = 8
VMEM_LIMIT_BYTES = 56 * 1024 * 1024

BF16 = jnp.bfloat16
F32 = jnp.float32


def _cparams(*semantics):
    return pltpu.CompilerParams(dimension_semantics=semantics, vmem_limit_bytes=VMEM_LIMIT_BYTES)


def _tile(n, want):
    t = min(n, want)
    while n % t:
        t //= 2
    return t


def _inproj_kernel(x_ref, w_ref, o_ref, *, sigmoid_blocks):
    acc = jnp.dot(x_ref[...], w_ref[...], preferred_element_type=F32)
    j = pl.program_id(1)
    is_sigmoid = (j >= sigmoid_blocks[0]) & (j < sigmoid_blocks[1])

    @pl.when(jnp.logical_not(is_sigmoid))
    def _():
        o_ref[...] = acc.astype(o_ref.dtype)

    @pl.when(is_sigmoid)
    def _():
        o_ref[...] = jax.nn.sigmoid(acc).astype(o_ref.dtype)


def _inproj(x_bf, w_bf, sigmoid_cols):
    t, d = x_bf.shape
    n = w_bf.shape[1]
    tm = _tile(t, 1024)
    tn = _tile(n, 1024)
    assert sigmoid_cols[0] % tn == 0 and sigmoid_cols[1] % tn == 0
    return pl.pallas_call(
        functools.partial(_inproj_kernel, sigmoid_blocks=(sigmoid_cols[0] // tn, sigmoid_cols[1] // tn)),
        grid=(t // tm, n // tn),
        in_specs=[pl.BlockSpec((tm, d), lambda i, j: (i, 0)),
                  pl.BlockSpec((d, tn), lambda i, j: (0, j))],
        out_specs=pl.BlockSpec((tm, tn), lambda i, j: (i, j)),
        out_shape=jax.ShapeDtypeStruct((t, n), BF16),
        compiler_params=_cparams("parallel", "arbitrary"),
        name="inproj",
    )(x_bf, w_bf)


def _split3(a):
    hi = a.astype(BF16)
    r1 = a - hi.astype(F32)
    mid = r1.astype(BF16)
    lo = (r1 - mid.astype(F32)).astype(BF16)
    return hi, mid, lo


def _gateprep_kernel(x_ref, w_ref, b_ref, col_ref, row_ref, *, heads):
    lk = x_ref.shape[0]
    g = jnp.dot(x_ref[...], w_ref[...], preferred_element_type=F32) + b_ref[...]
    logf = jax.nn.log_sigmoid(g)
    r = lax.broadcasted_iota(jnp.int32, (lk, lk), 0)
    c = lax.broadcasted_iota(jnp.int32, (lk, lk), 1)
    lower = (c <= r).astype(BF16)
    upper = (c >= r).astype(BF16)
    parts = _split3(logf)
    pre = sum(jnp.dot(lower, p, preferred_element_type=F32) for p in parts)
    suf = sum(jnp.dot(upper, p, preferred_element_type=F32) for p in parts)
    lane = lax.broadcasted_iota(jnp.int32, g.shape, 1)
    is_f_fwd = (lane >= heads) & (lane < 2 * heads)
    is_f_bwd = (lane >= 3 * heads) & (lane < 4 * heads)
    col = jnp.where(is_f_fwd, pre, jnp.where(is_f_bwd, suf, g))
    col_ref[...] = col
    row_ref[0] = col.T[: 4 * heads, :]


def _gateprep(x_bf, wg_bf, bias, lk, heads):
    t, d = x_bf.shape
    nchunks = t // lk
    return pl.pallas_call(
        functools.partial(_gateprep_kernel, heads=heads),
        grid=(nchunks,),
        in_specs=[pl.BlockSpec((lk, d), lambda i: (i, 0)),
                  pl.BlockSpec((d, LANES), lambda i: (0, 0)),
                  pl.BlockSpec((1, LANES), lambda i: (0, 0))],
        out_specs=[pl.BlockSpec((lk, LANES), lambda i: (i, 0)),
                   pl.BlockSpec((1, 4 * heads, lk), lambda i: (i, 0, 0))],
        out_shape=[jax.ShapeDtypeStruct((t, LANES), F32),
                   jax.ShapeDtypeStruct((nchunks, 4 * heads, lk), F32)],
        compiler_params=_cparams("parallel"),
        name="gateprep",
    )(x_bf, wg_bf, bias)


def _mlstm_chunk(q, k, v, i_col, b_col, i_row, b_row, ct, n, m_prev, *, reverse, scale):
    lk = q.shape[0]
    row = lax.broadcasted_iota(jnp.int32, (lk, lk), 0)
    col = lax.broadcasted_iota(jnp.int32, (lk, lk), 1)
    mask = (col >= row) if reverse else (col <= row)
    dmat = jnp.where(mask, b_col - b_row + i_row, -jnp.inf)
    inter = b_col + m_prev
    m_t = jnp.maximum(inter, jnp.max(dmat, axis=1, keepdims=True))
    qs = (q.astype(F32) * scale).astype(BF16)
    s = lax.dot_general(qs, k, (((1,), (1,)), ((), ())), preferred_element_type=F32)
    a = s * jnp.exp(dmat - m_t)
    w_inter = jnp.exp(inter - m_t)
    num = (jnp.dot(a.astype(BF16), v, preferred_element_type=F32)
           + w_inter * jnp.dot(qs, ct.astype(BF16), preferred_element_type=F32))
    qn = jnp.sum(qs.astype(F32) * n.astype(BF16).astype(F32), axis=1, keepdims=True)
    den = jnp.sum(a, axis=1, keepdims=True) + w_inter * qn
    h = num / jnp.maximum(jnp.abs(den), jnp.exp(-m_t))
    g = b_col[0:1, :] if reverse else b_col[lk - 1:lk, :]
    a_col = g - b_col + i_col
    m_new = jnp.maximum(g + m_prev, jnp.max(a_col, axis=0, keepdims=True))
    decay = jnp.exp(g + m_prev - m_new)
    wa = jnp.exp(a_col - m_new)
    vw = (v.astype(F32) * wa).astype(BF16)
    ct_new = decay * ct + lax.dot_general(k, vw, (((0,), (0,)), ((), ())), preferred_element_type=F32)
    n_new = decay * n + jnp.sum(wa.astype(BF16).astype(F32) * k.astype(F32), axis=0, keepdims=True)
    return h, ct_new, n_new, m_new


def _mlstm_kernel(*refs, heads, dh, reverse, scale):
    if reverse:
        (q_ref, k_ref, v_ref, pc_ref, pr_ref, hf_ref, o_ref, nw_ref, out_ref, ct_ref, n_ref, m_ref) = refs
    else:
        (q_ref, k_ref, v_ref, pc_ref, pr_ref, out_ref, ct_ref, n_ref, m_ref) = refs

    @pl.when(pl.program_id(1) == 0)
    def _():
        ct_ref[...] = jnp.zeros_like(ct_ref)
        n_ref[...] = jnp.zeros_like(n_ref)
        m_ref[...] = jnp.zeros_like(m_ref)

    ig = 2 * heads if reverse else 0
    fg = ig + heads
    pc = pc_ref[...]
    pr = pr_ref[0]
    for h in range(heads):
        sl = slice(h * dh, (h + 1) * dh)
        hh, ct_new, n_new, m_new = _mlstm_chunk(
            q_ref[:, sl], k_ref[:, sl], v_ref[:, sl],
            pc[:, ig + h:ig + h + 1], pc[:, fg + h:fg + h + 1],
            pr[ig + h:ig + h + 1, :], pr[fg + h:fg + h + 1, :],
            ct_ref[h], n_ref[h], m_ref[h], reverse=reverse, scale=scale)
        ct_ref[h] = ct_new
        n_ref[h] = n_new
        m_ref[h] = m_new
        if reverse:
            tot = hf_ref[:, sl] + hh
            mu = jnp.mean(tot, axis=1, keepdims=True)
            cen = tot - mu
            var = jnp.mean(cen * cen, axis=1, keepdims=True)
            hn = cen * lax.rsqrt(var + GN_EPS) * nw_ref[:, sl]
            out_ref[:, sl] = (jax.nn.sigmoid(o_ref[:, sl].astype(F32)) * hn).astype(out_ref.dtype)
        else:
            out_ref[:, sl] = hh


def _mlstm(proj, pc, pr, col_q, col_k, col_v, col_o, hf, norm_w, *, batch, seq, lk, heads, dh, reverse):
    t = batch * seq
    width = heads * dh
    nc = seq // lk
    chunk = (lambda b, c: b * nc + (nc - 1 - c)) if reverse else (lambda b, c: b * nc + c)
    tok_spec = lambda colblk: pl.BlockSpec((lk, width), lambda b, c: (chunk(b, c), colblk))
    in_specs = [tok_spec(col_q // width), tok_spec(col_k // width), tok_spec(col_v // width),
                pl.BlockSpec((lk, LANES), lambda b, c: (chunk(b, c), 0)),
                pl.BlockSpec((1, 4 * heads, lk), lambda b, c: (chunk(b, c), 0, 0))]
    args = [proj, proj, proj, pc, pr]
    if reverse:
        in_specs += [pl.BlockSpec((lk, width), lambda b, c: (chunk(b, c), 0)),
                     tok_spec(col_o // width),
                     pl.BlockSpec((1, width), lambda b, c: (0, 0))]
        args += [hf, proj, norm_w]
        out_dtype = BF16
    else:
        out_dtype = F32
    return pl.pallas_call(
        functools.partial(_mlstm_kernel, heads=heads, dh=dh, reverse=reverse, scale=dh ** -0.5),
        grid=(batch, nc),
        in_specs=in_specs,
        out_specs=pl.BlockSpec((lk, width), lambda b, c: (chunk(b, c), 0)),
        out_shape=jax.ShapeDtypeStruct((t, width), out_dtype),
        scratch_shapes=[pltpu.VMEM((heads, dh, dh), F32),
                        pltpu.VMEM((heads, 1, dh), F32),
                        pltpu.VMEM((heads, 1, 1), F32)],
        compiler_params=_cparams("parallel", "arbitrary"),
        name="mlstm_bwd" if reverse else "mlstm_fwd",
    )(*args)


def _pool_kernel(prev_ref, cur_ref, next_ref, w_ref, s_ref, o_ref, *, seq, gd):
    tm = cur_ref.shape[0]
    i = pl.program_id(0)
    pos0 = (i * tm) % seq
    at_start = pos0 == 0
    at_end = pos0 + tm == seq
    prev = jnp.where(at_start, 0.0, prev_ref[...].astype(F32))
    nxt = jnp.where(at_end, 0.0, next_ref[...].astype(F32))
    cur = cur_ref[...].astype(F32)
    ext = jnp.concatenate([prev, cur, nxt], axis=0)
    n_ext = ext.shape[0]
    pos = pos0 + lax.broadcasted_iota(jnp.int32, (tm, 1), 0)
    for grp, win in enumerate(POOL_WINDOWS):
        e = ext[:, grp * gd:(grp + 1) * gd]
        acc = e + pltpu.roll(e, 1, 0)
        half = 1
        while 2 * half < win:
            acc = pltpu.roll(acc, n_ext - half, 0) + pltpu.roll(acc, half, 0)
            half *= 2
        wsum = acc[POOL_HALO:POOL_HALO + tm]
        lo = jnp.maximum(pos - win // 2, 0)
        hi = jnp.minimum(pos + win // 2, seq)
        cnt = (hi - lo).astype(F32)
        diff = wsum / cnt - cur[:, grp * gd:(grp + 1) * gd]
        y = jnp.dot(diff.astype(BF16), w_ref[grp], preferred_element_type=F32)
        o_ref[:, grp * gd:(grp + 1) * gd] = (y * s_ref[:, grp * gd:(grp + 1) * gd]).astype(o_ref.dtype)


def _pool(proj, w_grp_bf, scale, *, seq, width, col_pool):
    t = proj.shape[0]
    ngrp, gd, _ = w_grp_bf.shape
    tm = _tile(seq, 512)
    hb = tm // POOL_HALO
    nhalo = t // POOL_HALO
    cb = col_pool // width
    return pl.pallas_call(
        functools.partial(_pool_kernel, seq=seq, gd=gd),
        grid=(t // tm,),
        in_specs=[pl.BlockSpec((POOL_HALO, width), lambda i: (jnp.maximum(i * hb - 1, 0), cb)),
                  pl.BlockSpec((tm, width), lambda i: (i, cb)),
                  pl.BlockSpec((POOL_HALO, width), lambda i: (jnp.minimum((i + 1) * hb, nhalo - 1), cb)),
                  pl.BlockSpec((ngrp, gd, gd), lambda i: (0, 0, 0)),
                  pl.BlockSpec((1, width), lambda i: (0, 0))],
        out_specs=pl.BlockSpec((tm, width), lambda i: (i, 0)),
        out_shape=jax.ShapeDtypeStruct((t, width), BF16),
        compiler_params=_cparams("parallel"),
        name="pool",
    )(proj, proj, proj, w_grp_bf, scale)


def _merge_kernel(p_ref, m_ref, wp_ref, wm_ref, gp_ref, gm_ref, o_ref):
    yp = jnp.dot(p_ref[...], wp_ref[...], preferred_element_type=F32)
    ym = jnp.dot(m_ref[...], wm_ref[...], preferred_element_type=F32)
    o_ref[...] = (gp_ref[...].astype(F32) * yp + gm_ref[...].astype(F32) * ym).astype(o_ref.dtype)


def _merge(p, m, wp_bf, wm_bf, proj, col_gp, col_gm):
    t, pw = p.shape
    mw = m.shape[1]
    d = wp_bf.shape[1]
    tm = _tile(t, 512)
    tn = _tile(d, 1024)
    return pl.pallas_call(
        _merge_kernel,
        grid=(t // tm, d // tn),
        in_specs=[pl.BlockSpec((tm, pw), lambda i, j: (i, 0)),
                  pl.BlockSpec((tm, mw), lambda i, j: (i, 0)),
                  pl.BlockSpec((pw, tn), lambda i, j: (0, j)),
                  pl.BlockSpec((mw, tn), lambda i, j: (0, j)),
                  pl.BlockSpec((tm, tn), lambda i, j: (i, col_gp // tn + j)),
                  pl.BlockSpec((tm, tn), lambda i, j: (i, col_gm // tn + j))],
        out_specs=pl.BlockSpec((tm, tn), lambda i, j: (i, j)),
        out_shape=jax.ShapeDtypeStruct((t, d), BF16),
        compiler_params=_cparams("parallel", "arbitrary"),
        name="merge",
    )(p, m, wp_bf, wm_bf, proj, proj)


def _layer_norm_rows(y, g, b):
    mu = jnp.mean(y, axis=1, keepdims=True)
    cen = y - mu
    var = jnp.mean(cen * cen, axis=1, keepdims=True)
    return cen * lax.rsqrt(var + LN_EPS) * g + b


def _outproj_kernel(mix_ref, w_ref, x_ref, g_ref, b_ref, wr_ref, xo_ref, xb_ref, aff_ref, *, alpha, experts):
    mix = jnp.dot(mix_ref[...], w_ref[...], preferred_element_type=F32)
    x1 = _layer_norm_rows(alpha * x_ref[...] + mix, g_ref[...], b_ref[...])
    xo_ref[...] = x1
    x1b = x1.astype(BF16)
    xb_ref[...] = x1b
    logits = jnp.dot(x1b, wr_ref[...], preferred_element_type=F32)
    lane = lax.broadcasted_iota(jnp.int32, logits.shape, 1)
    logits = jnp.where(lane < experts, logits, -jnp.inf)
    e = jnp.exp(logits - jnp.max(logits, axis=1, keepdims=True))
    aff_ref[...] = e / jnp.sum(e, axis=1, keepdims=True)


def _outproj(mixin, wout_bf, x, ln_g, ln_b, wr_bf, *, alpha, experts):
    t, d = x.shape
    tm = _tile(t, 256)
    row = lambda i: (i, 0)
    full = lambda i: (0, 0)
    return pl.pallas_call(
        functools.partial(_outproj_kernel, alpha=alpha, experts=experts),
        grid=(t // tm,),
        in_specs=[pl.BlockSpec((tm, d), row), pl.BlockSpec((d, d), full), pl.BlockSpec((tm, d), row),
                  pl.BlockSpec((1, d), full), pl.BlockSpec((1, d), full), pl.BlockSpec((d, LANES), full)],
        out_specs=[pl.BlockSpec((tm, d), row), pl.BlockSpec((tm, d), row), pl.BlockSpec((tm, LANES), row)],
        out_shape=[jax.ShapeDtypeStruct((t, d), F32), jax.ShapeDtypeStruct((t, d), BF16),
                   jax.ShapeDtypeStruct((t, LANES), F32)],
        compiler_params=_cparams("parallel"),
        name="outproj_ln",
    )(mixin, wout_bf, x, ln_g, ln_b, wr_bf)


def _expert_kernel(x_ref, gate_ref, wg_ref, wu_ref, wd_ref, y_ref):
    xe = x_ref[0, 0]
    g = jnp.dot(xe, wg_ref[0], preferred_element_type=F32)
    u = jnp.dot(xe, wu_ref[0], preferred_element_type=F32)
    hid = (g * jax.nn.sigmoid(g) * u).astype(BF16)
    y_ref[0, 0] = jnp.dot(hid, wd_ref[0], preferred_element_type=F32) * gate_ref[0, 0]


def _experts(xe, gate, wg_bf, wu_bf, wd_bf):
    b, e, c, d = xe.shape
    f = wg_bf.shape[2]
    tm = _tile(c, 256)
    tok = lambda ei, bi, ci: (bi, ei, ci, 0)
    wmap = lambda ei, bi, ci: (ei, 0, 0)
    return pl.pallas_call(
        _expert_kernel,
        grid=(e, b, c // tm),
        in_specs=[pl.BlockSpec((1, 1, tm, d), tok), pl.BlockSpec((1, 1, tm, 1), tok),
                  pl.BlockSpec((1, d, f), wmap), pl.BlockSpec((1, d, f), wmap), pl.BlockSpec((1, f, d), wmap)],
        out_specs=pl.BlockSpec((1, 1, tm, d), tok),
        out_shape=jax.ShapeDtypeStruct((b, e, c, d), F32),
        compiler_params=_cparams("parallel", "parallel", "arbitrary"),
        name="experts",
    )(xe, gate, wg_bf, wu_bf, wd_bf)


def _ln2_kernel(x_ref, f_ref, g_ref, b_ref, xo_ref, xb_ref, *, alpha):
    x2 = _layer_norm_rows(alpha * x_ref[...] + f_ref[...], g_ref[...], b_ref[...])
    xo_ref[...] = x2
    xb_ref[...] = x2.astype(BF16)


def _ln2(x, ffn, ln_g, ln_b, *, alpha):
    t, d = x.shape
    tm = _tile(t, 512)
    row = lambda i: (i, 0)
    full = lambda i: (0, 0)
    return pl.pallas_call(
        functools.partial(_ln2_kernel, alpha=alpha),
        grid=(t // tm,),
        in_specs=[pl.BlockSpec((tm, d), row), pl.BlockSpec((tm, d), row),
                  pl.BlockSpec((1, d), full), pl.BlockSpec((1, d), full)],
        out_specs=[pl.BlockSpec((tm, d), row), pl.BlockSpec((tm, d), row)],
        out_shape=[jax.ShapeDtypeStruct((t, d), F32), jax.ShapeDtypeStruct((t, d), BF16)],
        compiler_params=_cparams("parallel"),
        name="ln2",
    )(x, ffn, ln_g, ln_b)


def kernel(x, w_in, b_if, w_pool_grp, pool_scale, w_pool_up, mlstm_norm_w, w_mlstm_up, w_out, ln1_g, ln1_b,
           w_router, w_gate, w_up, w_down, ln2_g, ln2_b):
    batch, seq, d = x.shape
    depth = w_in.shape[0]
    t = batch * seq
    heads = MLSTM_HEADS
    pw = w_pool_up.shape[1]
    mw = w_mlstm_up.shape[1]
    dh = mw // heads
    experts = w_router.shape[2]
    cap = EC_CAPACITY_FACTOR * seq // experts
    ngates = 4 * heads
    alpha = (2.0 * depth) ** 0.25
    lk = _tile(seq, 256)
    ref_q = pw
    ref_if = pw + 4 * mw
    ref_gp = ref_if + ngates
    col_q = 0
    col_k = col_q + mw
    col_v = col_k + mw
    col_o = col_v + mw
    col_gp = col_o + mw
    col_gm = col_gp + d
    col_pool = col_gm + d

    xf = x.reshape(t, d)
    xb = xf.astype(BF16)
    for layer in range(depth):
        wl = w_in[layer]
        w_main = jnp.concatenate([wl[:, ref_q:ref_if], wl[:, ref_gp:], wl[:, :ref_q]], axis=1).astype(BF16)
        w_gates = jnp.pad(wl[:, ref_if:ref_gp], ((0, 0), (0, LANES - ngates))).astype(BF16)
        bias = jnp.pad(b_if[layer], (0, LANES - ngates)).reshape(1, LANES)

        proj = _inproj(xb, w_main, (col_gp, col_pool))
        pc, pr = _gateprep(xb, w_gates, bias, lk, heads)
        common = dict(batch=batch, seq=seq, lk=lk, heads=heads, dh=dh)
        hf = _mlstm(proj, pc, pr, col_q, col_k, col_v, col_o, None, None, reverse=False, **common)
        m = _mlstm(proj, pc, pr, col_q, col_k, col_v, col_o, hf, mlstm_norm_w[layer].reshape(1, mw),
                   reverse=True, **common)
        p = _pool(proj, w_pool_grp[layer].astype(BF16), pool_scale[layer].reshape(1, pw), seq=seq, width=pw,
                  col_pool=col_pool)
        mixin = _merge(p, m, w_pool_up[layer].astype(BF16), w_mlstm_up[layer].astype(BF16), proj, col_gp, col_gm)
        wr = jnp.pad(w_router[layer], ((0, 0), (0, LANES - experts))).astype(BF16)
        x1, x1b, aff = _outproj(mixin, w_out[layer].astype(BF16), xf, ln1_g[layer].reshape(1, d),
                                ln1_b[layer].reshape(1, d), wr, alpha=alpha, experts=experts)

        affe = aff[:, :experts].reshape(batch, seq, experts).transpose(0, 2, 1)
        gate, idx = lax.top_k(affe, cap)
        xe = jax.vmap(lambda xr, ib: xr[ib])(x1b.reshape(batch, seq, d), idx)
        y = _experts(xe, gate[..., None], w_gate[layer].astype(BF16), w_up[layer].astype(BF16),
                     w_down[layer].astype(BF16))
        ffn = jax.vmap(lambda ib, yb: jnp.zeros((seq, d), F32).at[ib.reshape(-1)].add(yb.reshape(-1, d)))(idx, y)
        xf, xb = _ln2(x1, ffn.reshape(t, d), ln2_g[layer].reshape(1, d), ln2_b[layer].reshape(1, d), alpha=alpha)
    return xf.reshape(batch, seq, d)
```

```python
import functools

import jax
import jax.numpy as jnp
from jax import lax
from jax.experimental import pallas as pl
from jax.experimental.pallas import tpu as pltpu

MLSTM_HEADS = 8
POOL_WINDOWS = (2, 4, 8, 16)
POOL_HALO = max(POOL_WINDOWS) // 2
EC_CAPACITY_FACTOR = 2
LN_EPS = 1e-5
GN_EPS = 1e-6

LANES = 128
SUBLANES = 8
VMEM_LIMIT_BYTES = 56 * 1024 * 1024

BF16 = jnp.bfloat16
F32 = jnp.float32


def _cparams(*semantics):
    return pltpu.CompilerParams(dimension_semantics=semantics, vmem_limit_bytes=VMEM_LIMIT_BYTES)


def _tile(n, want):
    t = min(n, want)
    while n % t:
        t //= 2
    return t


def _inproj_kernel(x_ref, w_ref, o_ref, *, sigmoid_blocks):
    acc = jnp.dot(x_ref[...], w_ref[...], preferred_element_type=F32)
    j = pl.program_id(1)
    is_sigmoid = (j >= sigmoid_blocks[0]) & (j < sigmoid_blocks[1])

    @pl.when(jnp.logical_not(is_sigmoid))
    def _():
        o_ref[...] = acc.astype(o_ref.dtype)

    @pl.when(is_sigmoid)
    def _():
        o_ref[...] = jax.nn.sigmoid(acc).astype(o_ref.dtype)


def _inproj(x_bf, w_bf, sigmoid_cols):
    t, d = x_bf.shape
    n = w_bf.shape[1]
    tm = _tile(t, 1024)
    tn = _tile(n, 1024)
    assert sigmoid_cols[0] % tn == 0 and sigmoid_cols[1] % tn == 0
    return pl.pallas_call(
        functools.partial(_inproj_kernel, sigmoid_blocks=(sigmoid_cols[0] // tn, sigmoid_cols[1] // tn)),
        grid=(t // tm, n // tn),
        in_specs=[pl.BlockSpec((tm, d), lambda i, j: (i, 0)),
                  pl.BlockSpec((d, tn), lambda i, j: (0, j))],
        out_specs=pl.BlockSpec((tm, tn), lambda i, j: (i, j)),
        out_shape=jax.ShapeDtypeStruct((t, n), BF16),
        compiler_params=_cparams("parallel", "arbitrary"),
        name="inproj",
    )(x_bf, w_bf)


def _split3(a):
    hi = a.astype(BF16)
    r1 = a - hi.astype(F32)
    mid = r1.astype(BF16)
    lo = (r1 - mid.astype(F32)).astype(BF16)
    return hi, mid, lo


def _gateprep_kernel(x_ref, w_ref, b_ref, col_ref, row_ref, *, heads):
    lk = x_ref.shape[0]
    g = jnp.dot(x_ref[...], w_ref[...], preferred_element_type=F32) + b_ref[...]
    logf = jax.nn.log_sigmoid(g)
    r = lax.broadcasted_iota(jnp.int32, (lk, lk), 0)
    c = lax.broadcasted_iota(jnp.int32, (lk, lk), 1)
    lower = (c <= r).astype(BF16)
    upper = (c >= r).astype(BF16)
    parts = _split3(logf)
    pre = sum(jnp.dot(lower, p, preferred_element_type=F32) for p in parts)
    suf = sum(jnp.dot(upper, p, preferred_element_type=F32) for p in parts)
    lane = lax.broadcasted_iota(jnp.int32, g.shape, 1)
    is_f_fwd = (lane >= heads) & (lane < 2 * heads)
    is_f_bwd = (lane >= 3 * heads) & (lane < 4 * heads)
    col = jnp.where(is_f_fwd, pre, jnp.where(is_f_bwd, suf, g))
    col_ref[...] = col
    row_ref[0] = col.T[: 4 * heads, :]


def _gateprep(x_bf, wg_bf, bias, lk, heads):
    t, d = x_bf.shape
    nchunks = t // lk
    return pl.pallas_call(
        functools.partial(_gateprep_kernel, heads=heads),
        grid=(nchunks,),
        in_specs=[pl.BlockSpec((lk, d), lambda i: (i, 0)),
                  pl.BlockSpec((d, LANES), lambda i: (0, 0)),
                  pl.BlockSpec((1, LANES), lambda i: (0, 0))],
        out_specs=[pl.BlockSpec((lk, LANES), lambda i: (i, 0)),
                   pl.BlockSpec((1, 4 * heads, lk), lambda i: (i, 0, 0))],
        out_shape=[jax.ShapeDtypeStruct((t, LANES), F32),
                   jax.ShapeDtypeStruct((nchunks, 4 * heads, lk), F32)],
        compiler_params=_cparams("parallel"),
        name="gateprep",
    )(x_bf, wg_bf, bias)


def _mlstm_chunk(q, k, v, i_col, b_col, i_row, b_row, ct, n, m_prev, *, reverse, scale):
    lk = q.shape[0]
    row = lax.broadcasted_iota(jnp.int32, (lk, lk), 0)
    col = lax.broadcasted_iota(jnp.int32, (lk, lk), 1)
    mask = (col >= row) if reverse else (col <= row)
    dmat = jnp.where(mask, b_col - b_row + i_row, -jnp.inf)
    inter = b_col + m_prev
    m_t = jnp.maximum(inter, jnp.max(dmat, axis=1, keepdims=True))
    qs = (q.astype(F32) * scale).astype(BF16)
    s = lax.dot_general(qs, k, (((1,), (1,)), ((), ())), preferred_element_type=F32)
    a = s * jnp.exp(dmat - m_t)
    w_inter = jnp.exp(inter - m_t)
    num = (jnp.dot(a.astype(BF16), v, preferred_element_type=F32)
           + w_inter * jnp.dot(qs, ct.astype(BF16), preferred_element_type=F32))
    qn = jnp.sum(qs.astype(F32) * n.astype(BF16).astype(F32), axis=1, keepdims=True)
    den = jnp.sum(a, axis=1, keepdims=True) + w_inter * qn
    h = num / jnp.maximum(jnp.abs(den), jnp.exp(-m_t))
    g = b_col[0:1, :] if reverse else b_col[lk - 1:lk, :]
    a_col = g - b_col + i_col
    m_new = jnp.maximum(g + m_prev, jnp.max(a_col, axis=0, keepdims=True))
    decay = jnp.exp(g + m_prev - m_new)
    wa = jnp.exp(a_col - m_new)
    vw = (v.astype(F32) * wa).astype(BF16)
    ct_new = decay * ct + lax.dot_general(k, vw, (((0,), (0,)), ((), ())), preferred_element_type=F32)
    n_new = decay * n + jnp.sum(wa.astype(BF16).astype(F32) * k.astype(F32), axis=0, keepdims=True)
    return h, ct_new, n_new, m_new


def _mlstm_kernel(*refs, heads, dh, reverse, scale):
    if reverse:
        (q_ref, k_ref, v_ref, pc_ref, pr_ref, hf_ref, o_ref, nw_ref, out_ref, ct_ref, n_ref, m_ref) = refs
    else:
        (q_ref, k_ref, v_ref, pc_ref, pr_ref, out_ref, ct_ref, n_ref, m_ref) = refs

    @pl.when(pl.program_id(1) == 0)
    def _():
        ct_ref[...] = jnp.zeros_like(ct_ref)
        n_ref[...] = jnp.zeros_like(n_ref)
        m_ref[...] = jnp.zeros_like(m_ref)

    ig = 2 * heads if reverse else 0
    fg = ig + heads
    pc = pc_ref[...]
    pr = pr_ref[0]
    for h in range(heads):
        sl = slice(h * dh, (h + 1) * dh)
        hh, ct_new, n_new, m_new = _mlstm_chunk(
            q_ref[:, sl], k_ref[:, sl], v_ref[:, sl],
            pc[:, ig + h:ig + h + 1], pc[:, fg + h:fg + h + 1],
            pr[ig + h:ig + h + 1, :], pr[fg + h:fg + h + 1, :],
            ct_ref[h], n_ref[h], m_ref[h], reverse=reverse, scale=scale)
        ct_ref[h] = ct_new
        n_ref[h] = n_new
        m_ref[h] = m_new
        if reverse:
            tot = hf_ref[:, sl] + hh
            mu = jnp.mean(tot, axis=1, keepdims=True)
            cen = tot - mu
            var = jnp.mean(cen * cen, axis=1, keepdims=True)
            hn = cen * lax.rsqrt(var + GN_EPS) * nw_ref[:, sl]
            out_ref[:, sl] = (jax.nn.sigmoid(o_ref[:, sl].astype(F32)) * hn).astype(out_ref.dtype)
        else:
            out_ref[:, sl] = hh


def _mlstm(proj, pc, pr, col_q, col_k, col_v, col_o, hf, norm_w, *, batch, seq, lk, heads, dh, reverse):
    t = batch * seq
    width = heads * dh
    nc = seq // lk
    chunk = (lambda b, c: b * nc + (nc - 1 - c)) if reverse else (lambda b, c: b * nc + c)
    tok_spec = lambda colblk: pl.BlockSpec((lk, width), lambda b, c: (chunk(b, c), colblk))
    in_specs = [tok_spec(col_q // width), tok_spec(col_k // width), tok_spec(col_v // width),
                pl.BlockSpec((lk, LANES), lambda b, c: (chunk(b, c), 0)),
                pl.BlockSpec((1, 4 * heads, lk), lambda b, c: (chunk(b, c), 0, 0))]
    args = [proj, proj, proj, pc, pr]
    if reverse:
        in_specs += [pl.BlockSpec((lk, width), lambda b, c: (chunk(b, c), 0)),
                     tok_spec(col_o // width),
                     pl.BlockSpec((1, width), lambda b, c: (0, 0))]
        args += [hf, proj, norm_w]
        out_dtype = BF16
    else:
        out_dtype = F32
    return pl.pallas_call(
        functools.partial(_mlstm_kernel, heads=heads, dh=dh, reverse=reverse, scale=dh ** -0.5),
        grid=(batch, nc),
        in_specs=in_specs,
        out_specs=pl.BlockSpec((lk, width), lambda b, c: (chunk(b, c), 0)),
        out_shape=jax.ShapeDtypeStruct((t, width), out_dtype),
        scratch_shapes=[pltpu.VMEM((heads, dh, dh), F32),
                        pltpu.VMEM((heads, 1, dh), F32),
                        pltpu.VMEM((heads, 1, 1), F32)],
        compiler_params=_cparams("parallel", "arbitrary"),
        name="mlstm_bwd" if reverse else "mlstm_fwd",
    )(*args)


def _pool_kernel(prev_ref, cur_ref, next_ref, w_ref, s_ref, o_ref, *, seq, gd):
    tm = cur_ref.shape[0]
    i = pl.program_id(0)
    pos0 = (i * tm) % seq
    at_start = pos0 == 0
    at_end = pos0 + tm == seq
    prev = jnp.where(at_start, 0.0, prev_ref[...].astype(F32))
    nxt = jnp.where(at_end, 0.0, next_ref[...].astype(F32))
    cur = cur_ref[...].astype(F32)
    ext = jnp.concatenate([prev, cur, nxt], axis=0)
    n_ext = ext.shape[0]
    pos = pos0 + lax.broadcasted_iota(jnp.int32, (tm, 1), 0)
    for grp, win in enumerate(POOL_WINDOWS):
        e = ext[:, grp * gd:(grp + 1) * gd]
        acc = e + pltpu.roll(e, 1, 0)
        half = 1
        while 2 * half < win:
            acc = pltpu.roll(acc, n_ext - half, 0) + pltpu.roll(acc, half, 0)
            half *= 2
        wsum = acc[POOL_HALO:POOL_HALO + tm]
        lo = jnp.maximum(pos - win // 2, 0)
        hi = jnp.minimum(pos + win // 2, seq)
        cnt = (hi - lo).astype(F32)
        diff = wsum / cnt - cur[:, grp * gd:(grp + 1) * gd]
        y = jnp.dot(diff.astype(BF16), w_ref[grp], preferred_element_type=F32)
        o_ref[:, grp * gd:(grp + 1) * gd] = (y * s_ref[:, grp * gd:(grp + 1) * gd]).astype(o_ref.dtype)


def _pool(proj, w_grp_bf, scale, *, seq, width, col_pool):
    t = proj.shape[0]
    ngrp, gd, _ = w_grp_bf.shape
    tm = _tile(seq, 512)
    hb = tm // POOL_HALO
    nhalo = t // POOL_HALO
    cb = col_pool // width
    return pl.pallas_call(
        functools.partial(_pool_kernel, seq=seq, gd=gd),
        grid=(t // tm,),
        in_specs=[pl.BlockSpec((POOL_HALO, width), lambda i: (jnp.maximum(i * hb - 1, 0), cb)),
                  pl.BlockSpec((tm, width), lambda i: (i, cb)),
                  pl.BlockSpec((POOL_HALO, width), lambda i: (jnp.minimum((i + 1) * hb, nhalo - 1), cb)),
                  pl.BlockSpec((ngrp, gd, gd), lambda i: (0, 0, 0)),
                  pl.BlockSpec((1, width), lambda i: (0, 0))],
        out_specs=pl.BlockSpec((tm, width), lambda i: (i, 0)),
        out_shape=jax.ShapeDtypeStruct((t, width), BF16),
        compiler_params=_cparams("parallel"),
        name="pool",
    )(proj, proj, proj, w_grp_bf, scale)


def _merge_kernel(p_ref, m_ref, wp_ref, wm_ref, gp_ref, gm_ref, o_ref):
    yp = jnp.dot(p_ref[...], wp_ref[...], preferred_element_type=F32)
    ym = jnp.dot(m_ref[...], wm_ref[...], preferred_element_type=F32)
    o_ref[...] = (gp_ref[...].astype(F32) * yp + gm_ref[...].astype(F32) * ym).astype(o_ref.dtype)


def _merge(p, m, wp_bf, wm_bf, proj, col_gp, col_gm):
    t, pw = p.shape
    mw = m.shape[1]
    d = wp_bf.shape[1]
    tm = _tile(t, 512)
    tn = _tile(d, 1024)
    return pl.pallas_call(
        _merge_kernel,
        grid=(t // tm, d // tn),
        in_specs=[pl.BlockSpec((tm, pw), lambda i, j: (i, 0)),
                  pl.BlockSpec((tm, mw), lambda i, j: (i, 0)),
                  pl.BlockSpec((pw, tn), lambda i, j: (0, j)),
                  pl.BlockSpec((mw, tn), lambda i, j: (0, j)),
                  pl.BlockSpec((tm, tn), lambda i, j: (i, col_gp // tn + j)),
                  pl.BlockSpec((tm, tn), lambda i, j: (i, col_gm // tn + j))],
        out_specs=pl.BlockSpec((tm, tn), lambda i, j: (i, j)),
        out_shape=jax.ShapeDtypeStruct((t, d), BF16),
        compiler_params=_cparams("parallel", "arbitrary"),
        name="merge",
    )(p, m, wp_bf, wm_bf, proj, proj)


def _layer_norm_rows(y, g, b):
    mu = jnp.mean(y, axis=1, keepdims=True)
    cen = y - mu
    var = jnp.mean(cen * cen, axis=1, keepdims=True)
    return cen * lax.rsqrt(var + LN_EPS) * g + b


def _outproj_kernel(mix_ref, w_ref, x_ref, g_ref, b_ref, wr_ref, xo_ref, xb_ref, aff_ref, *, alpha, experts):
    mix = jnp.dot(mix_ref[...], w_ref[...], preferred_element_type=F32)
    x1 = _layer_norm_rows(alpha * x_ref[...] + mix, g_ref[...], b_ref[...])
    xo_ref[...] = x1
    x1b = x1.astype(BF16)
    xb_ref[...] = x1b
    logits = jnp.dot(x1b, wr_ref[...], preferred_element_type=F32)
    lane = lax.broadcasted_iota(jnp.int32, logits.shape, 1)
    logits = jnp.where(lane < experts, logits, -jnp.inf)
    e = jnp.exp(logits - jnp.max(logits, axis=1, keepdims=True))
    aff_ref[...] = e / jnp.sum(e, axis=1, keepdims=True)


def _outproj(mixin, wout_bf, x, ln_g, ln_b, wr_bf, *, alpha, experts):
    t, d = x.shape
    tm = _tile(t, 256)
    row = lambda i: (i, 0)
    full = lambda i: (0, 0)
    return pl.pallas_call(
        functools.partial(_outproj_kernel, alpha=alpha, experts=experts),
        grid=(t // tm,),
        in_specs=[pl.BlockSpec((tm, d), row), pl.BlockSpec((d, d), full), pl.BlockSpec((tm, d), row),
                  pl.BlockSpec((1, d), full), pl.BlockSpec((1, d), full), pl.BlockSpec((d, LANES), full)],
        out_specs=[pl.BlockSpec((tm, d), row), pl.BlockSpec((tm, d), row), pl.BlockSpec((tm, LANES), row)],
        out_shape=[jax.ShapeDtypeStruct((t, d), F32), jax.ShapeDtypeStruct((t, d), BF16),
                   jax.ShapeDtypeStruct((t, LANES), F32)],
        compiler_params=_cparams("parallel"),
        name="outproj_ln",
    )(mixin, wout_bf, x, ln_g, ln_b, wr_bf)


ROUTE_TILE = 256
ROUTE_COUNT_CHUNK = 1024


def _route_kernel(aff_ref, pos_ref, p0_ref, *, cap, tm):
    s = aff_ref.shape[0]
    chunk = min(ROUTE_COUNT_CHUNK, s)

    def count_ge(cand):
        def body(c, acc):
            blk = aff_ref[pl.ds(pl.multiple_of(c * chunk, chunk), chunk), :]
            return acc + jnp.sum(jnp.where(blk >= cand, 1, 0), axis=0, keepdims=True)
        return lax.fori_loop(0, s // chunk, body, jnp.zeros((1, LANES), jnp.int32))

    def bit_body(k, thr_bits):
        cand = thr_bits | jnp.left_shift(jnp.int32(1), 30 - k)
        cnt = count_ge(lax.bitcast_convert_type(cand, F32))
        return jnp.where(cnt >= cap, cand, thr_bits)

    thr_bits = lax.fori_loop(0, 31, bit_body, jnp.zeros((1, LANES), jnp.int32))
    thr = lax.bitcast_convert_type(thr_bits, F32)
    n_gt = count_ge(lax.bitcast_convert_type(thr_bits + 1, F32))
    need = (cap - n_gt).astype(F32)

    r = lax.broadcasted_iota(jnp.int32, (tm, tm), 0)
    c = lax.broadcasted_iota(jnp.int32, (tm, tm), 1)
    tri = jnp.where(c < r, 1.0, 0.0).astype(BF16)

    def tile_body(j, carry):
        run_eq, run_sel = carry
        rows = pl.ds(pl.multiple_of(j * tm, tm), tm)
        blk = aff_ref[rows, :]
        gt = blk > thr
        eq = blk == thr
        eqf = jnp.where(eq, 1.0, 0.0)
        eq_rank = run_eq + jnp.dot(tri, eqf.astype(BF16), preferred_element_type=F32)
        sel = gt | (eq & (eq_rank < need))
        self_ = jnp.where(sel, 1.0, 0.0)
        pos = run_sel + jnp.dot(tri, self_.astype(BF16), preferred_element_type=F32)
        pos_ref[rows, :] = jnp.where(sel, pos, -1.0).astype(jnp.int32)
        p0_ref[0, pl.ds(j, 1), :] = run_sel.astype(jnp.int32)
        return (run_eq + jnp.sum(eqf, axis=0, keepdims=True), run_sel + jnp.sum(self_, axis=0, keepdims=True))

    zero = jnp.zeros((1, LANES), F32)
    lax.fori_loop(0, s // tm, tile_body, (zero, zero))


def _route(aff, *, batch, seq, cap, tm):
    t = aff.shape[0]
    ntiles = seq // tm
    return pl.pallas_call(
        functools.partial(_route_kernel, cap=cap, tm=tm),
        grid=(batch,),
        in_specs=[pl.BlockSpec((seq, LANES), lambda b: (b, 0))],
        out_specs=[pl.BlockSpec((seq, LANES), lambda b: (b, 0)),
                   pl.BlockSpec((1, ntiles, LANES), lambda b: (b, 0, 0))],
        out_shape=[jax.ShapeDtypeStruct((t, LANES), jnp.int32),
                   jax.ShapeDtypeStruct((batch, ntiles, LANES), jnp.int32)],
        compiler_params=_cparams("parallel"),
        name="route",
    )(aff)


SLOT_VALUES = 8
COMPACT_WINDOW = 128


def _compact_kernel(p0_ref, pos_ref, aff_ref, out_ref, *, experts, tm, ntiles, win):
    b = pl.program_id(0)
    j = pl.program_id(1)

    @pl.when(j == 0)
    def _():
        out_ref[...] = jnp.zeros_like(out_ref)

    post = pos_ref[...].astype(F32).T
    hi, mid, lo = _split3(aff_ref[...])
    rr = lax.broadcasted_iota(jnp.int32, (LANES, LANES), 0)
    cc = lax.broadcasted_iota(jnp.int32, (LANES, LANES), 1)
    spread = lambda k: jnp.where((cc == rr * SLOT_VALUES + k) & (rr < experts), 1.0, 0.0).astype(BF16)
    vals = (jnp.dot(hi, spread(2), preferred_element_type=F32)
            + jnp.dot(mid, spread(3), preferred_element_type=F32)
            + jnp.dot(lo, spread(4), preferred_element_type=F32))
    tok = j * tm + lax.broadcasted_iota(jnp.int32, (tm, LANES), 0)
    lane = lax.broadcasted_iota(jnp.int32, (tm, LANES), 1)
    k8 = lane % SLOT_VALUES
    vals = jnp.where(k8 == 0, (tok // LANES).astype(F32), jnp.where(k8 == 1, (tok % LANES).astype(F32), vals))
    vals = vals.astype(BF16)
    lane_expert = lane // SLOT_VALUES
    slot_iota = lax.broadcasted_iota(jnp.int32, (win, 1), 0)
    for e in range(experts):
        base = (b * (ntiles + 1) + j) * experts + e
        p0 = p0_ref[base]
        p1 = p0_ref[base + experts]
        a0 = (p0 // SUBLANES) * SUBLANES
        nwin = (p1 - a0 + win - 1) // win
        ve = jnp.where(lane_expert == e, vals, jnp.zeros_like(vals))
        prow = post[e:e + 1, :]

        def wbody(w, carry, a0=a0, ve=ve, prow=prow):
            start = pl.multiple_of(a0 + w * win, SUBLANES)
            onehot = jnp.where(prow == (start + slot_iota).astype(F32), 1.0, 0.0).astype(BF16)
            out_ref[0, pl.ds(start, win), :] += jnp.dot(onehot, ve, preferred_element_type=F32)
            return carry

        lax.fori_loop(0, nwin, wbody, 0)


def _compact(p0_flat, pos, aff, *, batch, seq, cap, experts, tm):
    ntiles = seq // tm
    win = COMPACT_WINDOW
    cpad = cap + win
    tok = lambda b, j, p0: (b * ntiles + j, 0)
    return pl.pallas_call(
        functools.partial(_compact_kernel, experts=experts, tm=tm, ntiles=ntiles, win=win),
        grid_spec=pltpu.PrefetchScalarGridSpec(
            num_scalar_prefetch=1,
            grid=(batch, ntiles),
            in_specs=[pl.BlockSpec((tm, LANES), tok), pl.BlockSpec((tm, LANES), tok)],
            out_specs=pl.BlockSpec((1, cpad, LANES), lambda b, j, p0: (b, 0, 0))),
        out_shape=jax.ShapeDtypeStruct((batch, cpad, LANES), F32),
        compiler_params=_cparams("parallel", "arbitrary"),
        name="compact",
    )(p0_flat, pos, aff)


def _expert_kernel(x_ref, gate_ref, wg_ref, wu_ref, wd_ref, y_ref):
    xe = x_ref[0, 0]
    g = jnp.dot(xe, wg_ref[0], preferred_element_type=F32)
    u = jnp.dot(xe, wu_ref[0], preferred_element_type=F32)
    hid = (g * jax.nn.sigmoid(g) * u).astype(BF16)
    y = jnp.dot(hid, wd_ref[0], preferred_element_type=F32) * gate_ref[0, 0]
    y_ref[0, 0] = y.astype(y_ref.dtype)


def _experts(xe, gate, wg_bf, wu_bf, wd_bf):
    b, e, c, d = xe.shape
    f = wg_bf.shape[2]
    tm = _tile(c, 256)
    tok = lambda ei, bi, ci: (bi, ei, ci, 0)
    wmap = lambda ei, bi, ci: (ei, 0, 0)
    return pl.pallas_call(
        _expert_kernel,
        grid=(e, b, c // tm),
        in_specs=[pl.BlockSpec((1, 1, tm, d), tok), pl.BlockSpec((1, 1, tm, 1), tok),
                  pl.BlockSpec((1, d, f), wmap), pl.BlockSpec((1, d, f), wmap), pl.BlockSpec((1, f, d), wmap)],
        out_specs=pl.BlockSpec((1, 1, tm, d), tok),
        out_shape=jax.ShapeDtypeStruct((b, e, c, d), BF16),
        compiler_params=_cparams("parallel", "parallel", "arbitrary"),
        name="experts",
    )(xe, gate, wg_bf, wu_bf, wd_bf)


COMBINE_SLAB = 64
POS_SPLIT_BITS = 6


def _combine_kernel(p0_ref, y_hbm, pos_ref, x_ref, g_ref, b_ref, spread_ref, xo_ref, xb_ref,
                    slab_ref, extra_ref, acc_ref, sem, sem_extra, *, alpha, experts, tm, ntiles, cap, slab):
    g = pl.program_id(0)
    nsteps = pl.num_programs(0)

    def first_slot(step, e):
        bb = step // ntiles
        p0 = p0_ref[(bb * (ntiles + 1) + step % ntiles) * experts + e]
        return bb, (p0 // SUBLANES) * SUBLANES

    def first_slab_copy(step, slot, e):
        bb, a0 = first_slot(step, e)
        return pltpu.make_async_copy(
            y_hbm.at[bb, e, pl.ds(pl.multiple_of(jnp.minimum(a0, cap - slab), SUBLANES), slab), :],
            slab_ref.at[slot, pl.ds(e * slab, slab), :], sem.at[slot])

    def fetch(step, slot):
        for e in range(experts):
            first_slab_copy(step, slot, e).start()

    @pl.when(g == 0)
    def _():
        fetch(0, 0)

    @pl.when(g + 1 < nsteps)
    def _():
        fetch(g + 1, (g + 1) % 2)

    cur = g % 2
    for e in range(experts):
        first_slab_copy(g, cur, e).wait()

    q = pos_ref[...] + 1
    q_hi = jnp.right_shift(q, POS_SPLIT_BITS).astype(F32).astype(BF16)
    q_lo = jnp.bitwise_and(q, (1 << POS_SPLIT_BITS) - 1).astype(F32).astype(BF16)
    qx = (jnp.dot(q_hi, spread_ref[...], preferred_element_type=F32) * float(1 << POS_SPLIT_BITS)
          + jnp.dot(q_lo, spread_ref[...], preferred_element_type=F32))
    col = lax.broadcasted_iota(jnp.int32, (1, experts * slab), 1)
    target = col % slab + 1
    for e in range(experts):
        _, a0 = first_slot(g, e)
        target = target + jnp.where(col // slab == e, jnp.minimum(a0, cap - slab), 0)
    onehot = jnp.where(qx == target.astype(F32), 1.0, 0.0).astype(BF16)
    acc_ref[...] = jnp.dot(onehot, slab_ref[cur], preferred_element_type=F32)

    slot_iota = lax.broadcasted_iota(jnp.int32, (1, slab), 1)
    for e in range(experts):
        bb, a0 = first_slot(g, e)
        p1 = p0_ref[(bb * (ntiles + 1) + g % ntiles + 1) * experts + e]
        nslab = (p1 - a0 + slab - 1) // slab
        pcol = pos_ref[:, e:e + 1]

        def more(w, carry, bb=bb, a0=a0, pcol=pcol, e=e):
            nominal = a0 + w * slab
            start = jnp.minimum(nominal, cap - slab)
            cp = pltpu.make_async_copy(y_hbm.at[bb, e, pl.ds(pl.multiple_of(start, SUBLANES), slab), :],
                                       extra_ref, sem_extra.at[0])
            cp.start()
            cp.wait()
            hit = (pcol == start + slot_iota) & (pcol >= nominal)
            acc_ref[...] += jnp.dot(jnp.where(hit, 1.0, 0.0).astype(BF16), extra_ref[...],
                                    preferred_element_type=F32)
            return carry

        lax.fori_loop(1, nslab, more, 0)

    x2 = _layer_norm_rows(alpha * x_ref[...] + acc_ref[...], g_ref[...], b_ref[...])
    xo_ref[...] = x2
    xb_ref[...] = x2.astype(BF16)


def _combine(p0_flat, y, pos, x, ln_g, ln_b, *, alpha, batch, seq, cap, experts, tm):
    t, d = x.shape
    ntiles = seq // tm
    slab = min(COMBINE_SLAB, cap)
    assert cap >> POS_SPLIT_BITS < 256 and experts <= LANES
    lane_expert = jnp.arange(experts * slab, dtype=jnp.int32)[None, :] // slab
    spread = (lane_expert == jnp.arange(LANES, dtype=jnp.int32)[:, None]).astype(BF16)
    row = lambda i, p0: (i, 0)
    full = lambda i, p0: (0, 0)
    return pl.pallas_call(
        functools.partial(_combine_kernel, alpha=alpha, experts=experts, tm=tm, ntiles=ntiles, cap=cap, slab=slab),
        grid_spec=pltpu.PrefetchScalarGridSpec(
            num_scalar_prefetch=1,
            grid=(t // tm,),
            in_specs=[pl.BlockSpec(memory_space=pl.ANY),
                      pl.BlockSpec((tm, LANES), row), pl.BlockSpec((tm, d), row),
                      pl.BlockSpec((1, d), full), pl.BlockSpec((1, d), full),
                      pl.BlockSpec((LANES, experts * slab), full)],
            out_specs=[pl.BlockSpec((tm, d), row), pl.BlockSpec((tm, d), row)],
            scratch_shapes=[pltpu.VMEM((2, experts * slab, d), BF16),
                            pltpu.VMEM((slab, d), BF16),
                            pltpu.VMEM((tm, d), F32),
                            pltpu.SemaphoreType.DMA((2,)),
                            pltpu.SemaphoreType.DMA((1,))]),
        out_shape=[jax.ShapeDtypeStruct((t, d), F32), jax.ShapeDtypeStruct((t, d), BF16)],
        compiler_params=_cparams("arbitrary"),
        name="combine_ln2",
    )(p0_flat, y, pos, x, ln_g, ln_b, spread)


def kernel(x, w_in, b_if, w_pool_grp, pool_scale, w_pool_up, mlstm_norm_w, w_mlstm_up, w_out, ln1_g, ln1_b,
           w_router, w_gate, w_up, w_down, ln2_g, ln2_b):
    batch, seq, d = x.shape
    depth = w_in.shape[0]
    t = batch * seq
    heads = MLSTM_HEADS
    pw = w_pool_up.shape[1]
    mw = w_mlstm_up.shape[1]
    dh = mw // heads
    experts = w_router.shape[2]
    cap = EC_CAPACITY_FACTOR * seq // experts
    ngates = 4 * heads
    alpha = (2.0 * depth) ** 0.25
    lk = _tile(seq, 256)
    rt = _tile(seq, ROUTE_TILE)
    ref_q = pw
    ref_if = pw + 4 * mw
    ref_gp = ref_if + ngates
    col_q = 0
    col_k = col_q + mw
    col_v = col_k + mw
    col_o = col_v + mw
    col_gp = col_o + mw
    col_gm = col_gp + d
    col_pool = col_gm + d

    xf = x.reshape(t, d)
    xb = xf.astype(BF16)
    for layer in range(depth):
        wl = w_in[layer]
        w_main = jnp.concatenate([wl[:, ref_q:ref_if], wl[:, ref_gp:], wl[:, :ref_q]], axis=1).astype(BF16)
        w_gates = jnp.concatenate([wl[:, ref_if:ref_gp], jnp.zeros((d, LANES - ngates), F32)], axis=1).astype(BF16)
        bias = jnp.concatenate([b_if[layer], jnp.zeros((LANES - ngates,), F32)]).reshape(1, LANES)

        proj = _inproj(xb, w_main, (col_gp, col_pool))
        pc, pr = _gateprep(xb, w_gates, bias, lk, heads)
        common = dict(batch=batch, seq=seq, lk=lk, heads=heads, dh=dh)
        hf = _mlstm(proj, pc, pr, col_q, col_k, col_v, col_o, None, None, reverse=False, **common)
        m = _mlstm(proj, pc, pr, col_q, col_k, col_v, col_o, hf, mlstm_norm_w[layer].reshape(1, mw),
                   reverse=True, **common)
        p = _pool(proj, w_pool_grp[layer].astype(BF16), pool_scale[layer].reshape(1, pw), seq=seq, width=pw,
                  col_pool=col_pool)
        mixin = _merge(p, m, w_pool_up[layer].astype(BF16), w_mlstm_up[layer].astype(BF16), proj, col_gp, col_gm)
        wr = jnp.concatenate([w_router[layer], jnp.zeros((d, LANES - experts), F32)], axis=1).astype(BF16)
        x1, x1b, aff = _outproj(mixin, w_out[layer].astype(BF16), xf, ln1_g[layer].reshape(1, d),
                                ln1_b[layer].reshape(1, d), wr, alpha=alpha, experts=experts)

        pos, p0 = _route(aff, batch=batch, seq=seq, cap=cap, tm=rt)
        p0_flat = jnp.concatenate([p0[:, :, :experts], jnp.full((batch, 1, experts), cap, jnp.int32)],
                                  axis=1).reshape(-1)
        slots = _compact(p0_flat, pos, aff, batch=batch, seq=seq, cap=cap, experts=experts, tm=rt)
        slots = slots[:, :cap, :].reshape(batch, cap, experts, SLOT_VALUES).transpose(0, 2, 1, 3)
        idx = (slots[..., 0] * LANES + slots[..., 1]).astype(jnp.int32)
        gate = slots[..., 2] + slots[..., 3] + slots[..., 4]
        flat_idx = (idx + (jnp.arange(batch, dtype=jnp.int32) * seq)[:, None, None]).reshape(-1)
        xe = jnp.take(x1b, flat_idx, axis=0).reshape(batch, experts, cap, d)
        y = _experts(xe, gate[..., None], w_gate[layer].astype(BF16), w_up[layer].astype(BF16),
                     w_down[layer].astype(BF16))
        xf, xb = _combine(p0_flat, y, pos, x1, ln2_g[layer].reshape(1, d), ln2_b[layer].reshape(1, d),
                          alpha=alpha, batch=batch, seq=seq, cap=cap, experts=experts, tm=rt)
    return xf.reshape(batch, seq, d)
```

```python
import functools

import jax
import jax.numpy as jnp
from jax import lax
from jax.experimental import pallas as pl
from jax.experimental.pallas import tpu as pltpu

MLSTM_HEADS = 8
POOL_WINDOWS = (2, 4, 8, 16)
POOL_HALO = max(POOL_WINDOWS) // 2
EC_CAPACITY_FACTOR = 2
LN_EPS = 1e-5
GN_EPS = 1e-6

LANES = 128
SUBLANES = 8
VMEM_LIMIT_BYTES = 56 * 1024 * 1024

BF16 = jnp.bfloat16
F32 = jnp.float32


def _cparams(*semantics):
    return pltpu.CompilerParams(dimension_semantics=semantics, vmem_limit_bytes=VMEM_LIMIT_BYTES)


def _tile(n, want):
    t = min(n, want)
    while n % t:
        t //= 2
    return t


def _inproj_kernel(x_ref, w_ref, o_ref, *, sigmoid_blocks):
    acc = jnp.dot(x_ref[...], w_ref[0], preferred_element_type=F32)
    j = pl.program_id(1)
    is_sigmoid = (j >= sigmoid_blocks[0]) & (j < sigmoid_blocks[1])

    @pl.when(jnp.logical_not(is_sigmoid))
    def _():
        o_ref[...] = acc.astype(o_ref.dtype)

    @pl.when(is_sigmoid)
    def _():
        o_ref[...] = jax.nn.sigmoid(acc).astype(o_ref.dtype)


def _inproj(x_bf, w_bf, layer, sigmoid_cols):
    t, d = x_bf.shape
    n = w_bf.shape[2]
    tm = _tile(t, 1024)
    tn = _tile(n, 1024)
    assert sigmoid_cols[0] % tn == 0 and sigmoid_cols[1] % tn == 0
    return pl.pallas_call(
        functools.partial(_inproj_kernel, sigmoid_blocks=(sigmoid_cols[0] // tn, sigmoid_cols[1] // tn)),
        grid=(t // tm, n // tn),
        in_specs=[pl.BlockSpec((tm, d), lambda i, j: (i, 0)),
                  pl.BlockSpec((1, d, tn), lambda i, j: (layer, 0, j))],
        out_specs=pl.BlockSpec((tm, tn), lambda i, j: (i, j)),
        out_shape=jax.ShapeDtypeStruct((t, n), BF16),
        compiler_params=_cparams("parallel", "arbitrary"),
        name="inproj",
    )(x_bf, w_bf)


def _split3(a):
    hi = a.astype(BF16)
    r1 = a - hi.astype(F32)
    mid = r1.astype(BF16)
    lo = (r1 - mid.astype(F32)).astype(BF16)
    return hi, mid, lo


def _gateprep_kernel(x_ref, w_ref, b_ref, col_ref, row_ref, *, heads):
    lk = x_ref.shape[0]
    g = jnp.dot(x_ref[...], w_ref[0], preferred_element_type=F32) + b_ref[0]
    logf = jax.nn.log_sigmoid(g)
    r = lax.broadcasted_iota(jnp.int32, (lk, lk), 0)
    c = lax.broadcasted_iota(jnp.int32, (lk, lk), 1)
    lower = (c <= r).astype(BF16)
    upper = (c >= r).astype(BF16)
    parts = _split3(logf)
    pre = sum(jnp.dot(lower, p, preferred_element_type=F32) for p in parts)
    suf = sum(jnp.dot(upper, p, preferred_element_type=F32) for p in parts)
    lane = lax.broadcasted_iota(jnp.int32, g.shape, 1)
    in_group = lambda k: (lane >= k * heads) & (lane < (k + 1) * heads)
    col = jnp.where(in_group(1), pre, jnp.where(in_group(3), suf, g))
    diff = col - pltpu.roll(col, LANES - heads, 1)
    row_id = lax.broadcasted_iota(jnp.int32, g.shape, 0)
    run_fwd = diff
    run_bwd = diff
    k = 1
    while k < lk:
        run_fwd = jnp.maximum(run_fwd, jnp.where(row_id >= k, pltpu.roll(run_fwd, k, 0), -jnp.inf))
        run_bwd = jnp.maximum(run_bwd, jnp.where(row_id < lk - k, pltpu.roll(run_bwd, lk - k, 0), -jnp.inf))
        k *= 2
    rowmax_fwd = pltpu.roll(col, 3 * heads, 1) + pltpu.roll(run_fwd, 4 * heads, 1)
    rowmax_bwd = pltpu.roll(col, 2 * heads, 1) + pltpu.roll(run_bwd, 3 * heads, 1)
    col = jnp.where(in_group(4), rowmax_fwd, jnp.where(in_group(5), rowmax_bwd, col))
    col_ref[...] = col
    row_ref[0] = col.T[: 4 * heads, :]


def _gateprep(x_bf, wg_bf, bias, layer, lk, heads):
    t, d = x_bf.shape
    nchunks = t // lk
    return pl.pallas_call(
        functools.partial(_gateprep_kernel, heads=heads),
        grid=(nchunks,),
        in_specs=[pl.BlockSpec((lk, d), lambda i: (i, 0)),
                  pl.BlockSpec((1, d, LANES), lambda i: (layer, 0, 0)),
                  pl.BlockSpec((1, 1, LANES), lambda i: (layer, 0, 0))],
        out_specs=[pl.BlockSpec((lk, LANES), lambda i: (i, 0)),
                   pl.BlockSpec((1, 4 * heads, lk), lambda i: (i, 0, 0))],
        out_shape=[jax.ShapeDtypeStruct((t, LANES), F32),
                   jax.ShapeDtypeStruct((nchunks, 4 * heads, lk), F32)],
        compiler_params=_cparams("parallel"),
        name="gateprep",
    )(x_bf, wg_bf, bias)


def _mlstm_chunk(q, k, v, i_col, b_col, rowmax_col, i_row, b_row, ct, n, m_prev, *, reverse):
    lk = q.shape[0]
    row = lax.broadcasted_iota(jnp.int32, (lk, lk), 0)
    col = lax.broadcasted_iota(jnp.int32, (lk, lk), 1)
    mask = (col >= row) if reverse else (col <= row)
    inter = b_col + m_prev
    m_t = jnp.maximum(inter, rowmax_col)
    w_inter = jnp.exp(inter - m_t)
    gates = jnp.where(mask, jnp.exp((b_col - m_t) + (i_row - b_row)), 0.0)
    a = gates * lax.dot_general(q, k, (((1,), (1,)), ((), ())), preferred_element_type=F32)
    num = (jnp.dot(a.astype(BF16), v, preferred_element_type=F32)
           + w_inter * jnp.dot(q, ct.astype(BF16), preferred_element_type=F32))
    qn = jnp.sum(q.astype(F32) * n.astype(BF16).astype(F32), axis=1, keepdims=True)
    den = jnp.sum(a, axis=1, keepdims=True) + w_inter * qn
    h = num * (1.0 / jnp.maximum(jnp.abs(den), jnp.exp(-m_t)))
    g = b_col[0:1, :] if reverse else b_col[lk - 1:lk, :]
    a_col = g - b_col + i_col
    m_new = jnp.maximum(g + m_prev, jnp.max(a_col, axis=0, keepdims=True))
    decay = jnp.exp(g + m_prev - m_new)
    wa = jnp.exp(a_col - m_new)
    vw = (v.astype(F32) * wa).astype(BF16)
    ct_new = decay * ct + lax.dot_general(k, vw, (((0,), (0,)), ((), ())), preferred_element_type=F32)
    n_new = decay * n + jnp.sum(wa.astype(BF16).astype(F32) * k.astype(F32), axis=0, keepdims=True)
    return h, ct_new, n_new, m_new


def _mlstm_kernel(*refs, heads, dh, reverse):
    if reverse:
        (q_ref, k_ref, v_ref, pc_ref, pr_ref, hf_ref, o_ref, nw_ref, out_ref, ct_ref, n_ref, m_ref) = refs
    else:
        (q_ref, k_ref, v_ref, pc_ref, pr_ref, out_ref, ct_ref, n_ref, m_ref) = refs

    @pl.when(pl.program_id(1) == 0)
    def _():
        ct_ref[...] = jnp.zeros_like(ct_ref)
        n_ref[...] = jnp.zeros_like(n_ref)
        m_ref[...] = jnp.zeros_like(m_ref)

    ig = 2 * heads if reverse else 0
    fg = ig + heads
    mg = (5 if reverse else 4) * heads
    pc = pc_ref[...]
    pr = pr_ref[0]
    for h in range(heads):
        sl = slice(h * dh, (h + 1) * dh)
        hh, ct_new, n_new, m_new = _mlstm_chunk(
            q_ref[:, sl], k_ref[:, sl], v_ref[:, sl],
            pc[:, ig + h:ig + h + 1], pc[:, fg + h:fg + h + 1], pc[:, mg + h:mg + h + 1],
            pr[ig + h:ig + h + 1, :], pr[fg + h:fg + h + 1, :],
            ct_ref[h], n_ref[h], m_ref[h], reverse=reverse)
        ct_ref[h] = ct_new
        n_ref[h] = n_new
        m_ref[h] = m_new
        if reverse:
            tot = hf_ref[:, sl] + hh
            mu = jnp.mean(tot, axis=1, keepdims=True)
            cen = tot - mu
            var = jnp.mean(cen * cen, axis=1, keepdims=True)
            hn = cen * lax.rsqrt(var + GN_EPS) * nw_ref[0, :, sl]
            out_ref[:, sl] = (jax.nn.sigmoid(o_ref[:, sl].astype(F32)) * hn).astype(out_ref.dtype)
        else:
            out_ref[:, sl] = hh


def _mlstm(proj, pc, pr, col_q, col_k, col_v, col_o, hf, norm_w, layer, *, batch, seq, lk, heads, dh, reverse):
    t = batch * seq
    width = heads * dh
    nc = seq // lk
    chunk = (lambda b, c: b * nc + (nc - 1 - c)) if reverse else (lambda b, c: b * nc + c)
    tok_spec = lambda colblk: pl.BlockSpec((lk, width), lambda b, c: (chunk(b, c), colblk))
    in_specs = [tok_spec(col_q // width), tok_spec(col_k // width), tok_spec(col_v // width),
                pl.BlockSpec((lk, LANES), lambda b, c: (chunk(b, c), 0)),
                pl.BlockSpec((1, 4 * heads, lk), lambda b, c: (chunk(b, c), 0, 0))]
    args = [proj, proj, proj, pc, pr]
    if reverse:
        in_specs += [pl.BlockSpec((lk, width), lambda b, c: (chunk(b, c), 0)),
                     tok_spec(col_o // width),
                     pl.BlockSpec((1, 1, width), lambda b, c: (layer, 0, 0))]
        args += [hf, proj, norm_w]
        out_dtype = BF16
    else:
        out_dtype = F32
    return pl.pallas_call(
        functools.partial(_mlstm_kernel, heads=heads, dh=dh, reverse=reverse),
        grid=(batch, nc),
        in_specs=in_specs,
        out_specs=pl.BlockSpec((lk, width), lambda b, c: (chunk(b, c), 0)),
        out_shape=jax.ShapeDtypeStruct((t, width), out_dtype),
        scratch_shapes=[pltpu.VMEM((heads, dh, dh), F32),
                        pltpu.VMEM((heads, 1, dh), F32),
                        pltpu.VMEM((heads, 1, 1), F32)],
        compiler_params=_cparams("parallel", "arbitrary"),
        name="mlstm_bwd" if reverse else "mlstm_fwd",
    )(*args)


def _pool_kernel(prev_ref, cur_ref, next_ref, w_ref, s_ref, o_ref, *, seq, gd):
    tm = cur_ref.shape[0]
    i = pl.program_id(0)
    pos0 = (i * tm) % seq
    at_start = pos0 == 0
    at_end = pos0 + tm == seq
    prev = jnp.where(at_start, 0.0, prev_ref[...].astype(F32))
    nxt = jnp.where(at_end, 0.0, next_ref[...].astype(F32))
    cur = cur_ref[...].astype(F32)
    ext = jnp.concatenate([prev, cur, nxt], axis=0)
    n_ext = ext.shape[0]
    pos = pos0 + lax.broadcasted_iota(jnp.int32, (tm, 1), 0)
    for grp, win in enumerate(POOL_WINDOWS):
        e = ext[:, grp * gd:(grp + 1) * gd]
        acc = e + pltpu.roll(e, 1, 0)
        half = 1
        while 2 * half < win:
            acc = pltpu.roll(acc, n_ext - half, 0) + pltpu.roll(acc, half, 0)
            half *= 2
        wsum = acc[POOL_HALO:POOL_HALO + tm]
        lo = jnp.maximum(pos - win // 2, 0)
        hi = jnp.minimum(pos + win // 2, seq)
        cnt = (hi - lo).astype(F32)
        diff = wsum / cnt - cur[:, grp * gd:(grp + 1) * gd]
        y = jnp.dot(diff.astype(BF16), w_ref[0, grp], preferred_element_type=F32)
        o_ref[:, grp * gd:(grp + 1) * gd] = (y * s_ref[0, :, grp * gd:(grp + 1) * gd]).astype(o_ref.dtype)


def _pool(proj, w_grp_bf, scale, layer, *, seq, width, col_pool):
    t = proj.shape[0]
    _, ngrp, gd, _ = w_grp_bf.shape
    tm = _tile(seq, 512)
    hb = tm // POOL_HALO
    nhalo = t // POOL_HALO
    cb = col_pool // width
    return pl.pallas_call(
        functools.partial(_pool_kernel, seq=seq, gd=gd),
        grid=(t // tm,),
        in_specs=[pl.BlockSpec((POOL_HALO, width), lambda i: (jnp.maximum(i * hb - 1, 0), cb)),
                  pl.BlockSpec((tm, width), lambda i: (i, cb)),
                  pl.BlockSpec((POOL_HALO, width), lambda i: (jnp.minimum((i + 1) * hb, nhalo - 1), cb)),
                  pl.BlockSpec((1, ngrp, gd, gd), lambda i: (layer, 0, 0, 0)),
                  pl.BlockSpec((1, 1, width), lambda i: (layer, 0, 0))],
        out_specs=pl.BlockSpec((tm, width), lambda i: (i, 0)),
        out_shape=jax.ShapeDtypeStruct((t, width), BF16),
        compiler_params=_cparams("parallel"),
        name="pool",
    )(proj, proj, proj, w_grp_bf, scale)


def _merge_kernel(p_ref, m_ref, wp_ref, wm_ref, gp_ref, gm_ref, o_ref):
    yp = jnp.dot(p_ref[...], wp_ref[0], preferred_element_type=F32)
    ym = jnp.dot(m_ref[...], wm_ref[0], preferred_element_type=F32)
    o_ref[...] = (gp_ref[...].astype(F32) * yp + gm_ref[...].astype(F32) * ym).astype(o_ref.dtype)


def _merge(p, m, wp_bf, wm_bf, layer, proj, col_gp, col_gm):
    t, pw = p.shape
    mw = m.shape[1]
    d = wp_bf.shape[2]
    tm = _tile(t, 512)
    tn = _tile(d, 1024)
    return pl.pallas_call(
        _merge_kernel,
        grid=(t // tm, d // tn),
        in_specs=[pl.BlockSpec((tm, pw), lambda i, j: (i, 0)),
                  pl.BlockSpec((tm, mw), lambda i, j: (i, 0)),
                  pl.BlockSpec((1, pw, tn), lambda i, j: (layer, 0, j)),
                  pl.BlockSpec((1, mw, tn), lambda i, j: (layer, 0, j)),
                  pl.BlockSpec((tm, tn), lambda i, j: (i, col_gp // tn + j)),
                  pl.BlockSpec((tm, tn), lambda i, j: (i, col_gm // tn + j))],
        out_specs=pl.BlockSpec((tm, tn), lambda i, j: (i, j)),
        out_shape=jax.ShapeDtypeStruct((t, d), BF16),
        compiler_params=_cparams("parallel", "arbitrary"),
        name="merge",
    )(p, m, wp_bf, wm_bf, proj, proj)


def _layer_norm_rows(y, g, b):
    mu = jnp.mean(y, axis=1, keepdims=True)
    cen = y - mu
    var = jnp.mean(cen * cen, axis=1, keepdims=True)
    return cen * lax.rsqrt(var + LN_EPS) * g + b


def _pack_rows(x_bf, xp_ref):
    tm, d = x_bf.shape
    half = d // 2
    nchunks = half // LANES
    bits = lax.bitcast_convert_type(x_bf.astype(F32), jnp.uint32)
    for j in range(nchunks):
        lo = jnp.right_shift(bits[:, j * LANES:(j + 1) * LANES], jnp.uint32(16))
        hi = jnp.bitwise_and(bits[:, half + j * LANES:half + (j + 1) * LANES], jnp.uint32(0xFFFF0000))
        xp_ref[pl.ds(j, tm, stride=nchunks), :] = lax.bitcast_convert_type(lo | hi, jnp.int32)


def _unpack_rows(buf_ref, tm, nchunks):
    lo, hi = [], []
    for j in range(nchunks):
        w = buf_ref[pl.ds(j, tm, stride=nchunks), :]
        lo.append(lax.bitcast_convert_type(jnp.left_shift(w, 16), F32).astype(BF16))
        hi.append(lax.bitcast_convert_type(jnp.bitwise_and(w, jnp.int32(-65536)), F32).astype(BF16))
    return jnp.concatenate(lo + hi, axis=1)


def _outproj_kernel(mix_ref, w_ref, x_ref, g_ref, b_ref, wr_ref, xo_ref, xp_ref, aff_ref, *, alpha, experts):
    mix = jnp.dot(mix_ref[...], w_ref[0], preferred_element_type=F32)
    x1 = _layer_norm_rows(alpha * x_ref[...] + mix, g_ref[0], b_ref[0])
    xo_ref[...] = x1
    x1b = x1.astype(BF16)
    _pack_rows(x1b, xp_ref)
    logits = jnp.dot(x1b, wr_ref[0], preferred_element_type=F32)
    lane = lax.broadcasted_iota(jnp.int32, logits.shape, 1)
    logits = jnp.where(lane < experts, logits, -jnp.inf)
    e = jnp.exp(logits - jnp.max(logits, axis=1, keepdims=True))
    aff_ref[...] = e / jnp.sum(e, axis=1, keepdims=True)


def _outproj(mixin, wout_bf, x, ln_g, ln_b, wr_bf, layer, *, alpha, experts):
    t, d = x.shape
    tm = _tile(t, 256)
    pack_rows = d // (2 * LANES)
    assert pack_rows == SUBLANES, "one (8,128) int32 tile per packed token row"
    row = lambda i: (i, 0)
    lay = lambda i: (layer, 0, 0)
    return pl.pallas_call(
        functools.partial(_outproj_kernel, alpha=alpha, experts=experts),
        grid=(t // tm,),
        in_specs=[pl.BlockSpec((tm, d), row), pl.BlockSpec((1, d, d), lay), pl.BlockSpec((tm, d), row),
                  pl.BlockSpec((1, 1, d), lay), pl.BlockSpec((1, 1, d), lay), pl.BlockSpec((1, d, LANES), lay)],
        out_specs=[pl.BlockSpec((tm, d), row), pl.BlockSpec((tm * pack_rows, LANES), row),
                   pl.BlockSpec((tm, LANES), row)],
        out_shape=[jax.ShapeDtypeStruct((t, d), F32), jax.ShapeDtypeStruct((t * pack_rows, LANES), jnp.int32),
                   jax.ShapeDtypeStruct((t, LANES), F32)],
        compiler_params=_cparams("parallel"),
        name="outproj_ln",
    )(mixin, wout_bf, x, ln_g, ln_b, wr_bf)


ROUTE_TILE = 256
ROUTE_COUNT_CHUNK = 1024


def _route_kernel(aff_ref, pos_ref, p0_ref, *, cap, tm):
    s = aff_ref.shape[0]
    chunk = min(ROUTE_COUNT_CHUNK, s)

    def count_ge(cand):
        def body(c, acc):
            blk = aff_ref[pl.ds(pl.multiple_of(c * chunk, chunk), chunk), :]
            return acc + jnp.sum(jnp.where(blk >= cand, 1, 0), axis=0, keepdims=True)
        return lax.fori_loop(0, s // chunk, body, jnp.zeros((1, LANES), jnp.int32))

    def bit_body(k, thr_bits):
        cand = thr_bits | jnp.left_shift(jnp.int32(1), 30 - k)
        cnt = count_ge(lax.bitcast_convert_type(cand, F32))
        return jnp.where(cnt >= cap, cand, thr_bits)

    thr_bits = lax.fori_loop(0, 31, bit_body, jnp.zeros((1, LANES), jnp.int32))
    thr = lax.bitcast_convert_type(thr_bits, F32)
    n_gt = count_ge(lax.bitcast_convert_type(thr_bits + 1, F32))
    need = (cap - n_gt).astype(F32)

    r = lax.broadcasted_iota(jnp.int32, (tm, tm), 0)
    c = lax.broadcasted_iota(jnp.int32, (tm, tm), 1)
    tri = jnp.where(c < r, 1.0, 0.0).astype(BF16)

    def tile_body(j, carry):
        run_eq, run_sel = carry
        rows = pl.ds(pl.multiple_of(j * tm, tm), tm)
        blk = aff_ref[rows, :]
        gt = blk > thr
        eq = blk == thr
        eqf = jnp.where(eq, 1.0, 0.0)
        eq_rank = run_eq + jnp.dot(tri, eqf.astype(BF16), preferred_element_type=F32)
        sel = gt | (eq & (eq_rank < need))
        self_ = jnp.where(sel, 1.0, 0.0)
        pos = run_sel + jnp.dot(tri, self_.astype(BF16), preferred_element_type=F32)
        pos_ref[rows, :] = jnp.where(sel, pos, -1.0).astype(jnp.int32)
        p0_ref[0, pl.ds(j, 1), :] = run_sel.astype(jnp.int32)
        return (run_eq + jnp.sum(eqf, axis=0, keepdims=True), run_sel + jnp.sum(self_, axis=0, keepdims=True))

    zero = jnp.zeros((1, LANES), F32)
    lax.fori_loop(0, s // tm, tile_body, (zero, zero))


def _route(aff, *, batch, seq, cap, tm):
    t = aff.shape[0]
    ntiles = seq // tm
    return pl.pallas_call(
        functools.partial(_route_kernel, cap=cap, tm=tm),
        grid=(batch,),
        in_specs=[pl.BlockSpec((seq, LANES), lambda b: (b, 0))],
        out_specs=[pl.BlockSpec((seq, LANES), lambda b: (b, 0)),
                   pl.BlockSpec((1, ntiles, LANES), lambda b: (b, 0, 0))],
        out_shape=[jax.ShapeDtypeStruct((t, LANES), jnp.int32),
                   jax.ShapeDtypeStruct((batch, ntiles, LANES), jnp.int32)],
        compiler_params=_cparams("parallel"),
        name="route",
    )(aff)


SLOT_VALUES = 8
COMPACT_WINDOW = 128


def _compact_kernel(p0_ref, pos_ref, aff_ref, out_ref, *, experts, tm, ntiles, win):
    b = pl.program_id(0)
    j = pl.program_id(1)

    @pl.when(j == 0)
    def _():
        out_ref[...] = jnp.zeros_like(out_ref)

    post = pos_ref[...].astype(F32).T
    hi, mid, lo = _split3(aff_ref[...])
    rr = lax.broadcasted_iota(jnp.int32, (LANES, LANES), 0)
    cc = lax.broadcasted_iota(jnp.int32, (LANES, LANES), 1)
    spread = lambda k: jnp.where((cc == rr * SLOT_VALUES + k) & (rr < experts), 1.0, 0.0).astype(BF16)
    vals = (jnp.dot(hi, spread(2), preferred_element_type=F32)
            + jnp.dot(mid, spread(3), preferred_element_type=F32)
            + jnp.dot(lo, spread(4), preferred_element_type=F32))
    tok = j * tm + lax.broadcasted_iota(jnp.int32, (tm, LANES), 0)
    lane = lax.broadcasted_iota(jnp.int32, (tm, LANES), 1)
    k8 = lane % SLOT_VALUES
    vals = jnp.where(k8 == 0, (tok // LANES).astype(F32), jnp.where(k8 == 1, (tok % LANES).astype(F32), vals))
    vals = vals.astype(BF16)
    lane_expert = lane // SLOT_VALUES
    slot_iota = lax.broadcasted_iota(jnp.int32, (win, 1), 0)
    crowded = []
    for e in range(experts):
        base = (b * (ntiles + 1) + j) * experts + e
        p0 = p0_ref[base]
        p1 = p0_ref[base + experts]
        a0 = (p0 // SUBLANES) * SUBLANES
        nwin = (p1 - a0 + win - 1) // win
        ve = jnp.where(lane_expert == e, vals, jnp.zeros_like(vals))
        prow = post[e:e + 1, :]

        def add_window(w, carry, a0=a0, ve=ve, prow=prow):
            start = pl.multiple_of(a0 + w * win, SUBLANES)
            onehot = jnp.where(prow == (start + slot_iota).astype(F32), 1.0, 0.0).astype(BF16)
            out_ref[0, pl.ds(start, win), :] += jnp.dot(onehot, ve, preferred_element_type=F32)
            return carry

        add_window(0, 0)
        crowded.append((nwin, add_window))

    for nwin, add_window in crowded:
        lax.fori_loop(1, nwin, add_window, 0)


def _compact(p0_flat, pos, aff, *, batch, seq, cap, experts, tm):
    ntiles = seq // tm
    win = COMPACT_WINDOW
    cpad = cap + win
    tok = lambda b, j, p0: (b * ntiles + j, 0)
    return pl.pallas_call(
        functools.partial(_compact_kernel, experts=experts, tm=tm, ntiles=ntiles, win=win),
        grid_spec=pltpu.PrefetchScalarGridSpec(
            num_scalar_prefetch=1,
            grid=(batch, ntiles),
            in_specs=[pl.BlockSpec((tm, LANES), tok), pl.BlockSpec((tm, LANES), tok)],
            out_specs=pl.BlockSpec((1, cpad, LANES), lambda b, j, p0: (b, 0, 0))),
        out_shape=jax.ShapeDtypeStruct((batch, cpad, LANES), F32),
        compiler_params=_cparams("parallel", "arbitrary"),
        name="compact",
    )(p0_flat, pos, aff)


EXPERT_TILE = 256


def _expert_kernel(idx_ref, idx_next_ref, xp_hbm, slots_ref, wg_ref, wu_ref, wd_ref, y_ref, buf_ref, sem,
                   *, tm, pack_rows):
    e = pl.program_id(0)
    step = (e * pl.num_programs(1) + pl.program_id(1)) * pl.num_programs(2) + pl.program_id(2)
    last = pl.num_programs(0) * pl.num_programs(1) * pl.num_programs(2) - 1

    def start_rows(ids_ref, slot):
        for i in range(tm):
            src = xp_hbm.at[pl.ds(pl.multiple_of(ids_ref[0, 0, i] * pack_rows, pack_rows), pack_rows), :]
            pltpu.make_async_copy(src, buf_ref.at[slot, pl.ds(i * pack_rows, pack_rows), :], sem.at[slot]).start()

    def wait_rows(slot):
        pltpu.make_async_copy(xp_hbm.at[pl.ds(0, tm * pack_rows), :], buf_ref.at[slot], sem.at[slot]).wait()

    @pl.when(step == 0)
    def _():
        start_rows(idx_ref, 0)

    cur = step % 2
    start_rows(idx_next_ref, 1 - cur)
    wait_rows(cur)

    xe = _unpack_rows(buf_ref.at[cur], tm, pack_rows)
    g = jnp.dot(xe, wg_ref[0, 0], preferred_element_type=F32)
    u = jnp.dot(xe, wu_ref[0, 0], preferred_element_type=F32)
    hid = (g * jax.nn.sigmoid(g) * u).astype(BF16)
    slots = slots_ref[0]
    lane = lax.broadcasted_iota(jnp.int32, slots.shape, 1)
    is_gate = (lane // SLOT_VALUES == e) & (lane % SLOT_VALUES >= 2) & (lane % SLOT_VALUES <= 4)
    gate = jnp.sum(jnp.where(is_gate, slots, 0.0), axis=1, keepdims=True)
    y = jnp.dot(hid, wd_ref[0, 0], preferred_element_type=F32) * gate
    y_ref[0, 0] = y.astype(y_ref.dtype)

    @pl.when(step == last)
    def _():
        wait_rows(1 - cur)


def _experts(idx_tiles, xp, slots, wg_bf, wu_bf, wd_bf, layer, *, batch, experts, cap, d, tm):
    f = wg_bf.shape[3]
    nct = cap // tm
    nsteps = experts * batch * nct
    pack_rows = d // (2 * LANES)
    step = lambda ei, bi, ci: (ei * batch + bi) * nct + ci
    wmap = lambda ei, bi, ci: (layer, ei, 0, 0)
    return pl.pallas_call(
        functools.partial(_expert_kernel, tm=tm, pack_rows=pack_rows),
        grid=(experts, batch, nct),
        in_specs=[pl.BlockSpec((1, 1, tm), lambda ei, bi, ci: (step(ei, bi, ci), 0, 0), memory_space=pltpu.SMEM),
                  pl.BlockSpec((1, 1, tm), lambda ei, bi, ci: (jnp.minimum(step(ei, bi, ci) + 1, nsteps - 1), 0, 0),
                               memory_space=pltpu.SMEM),
                  pl.BlockSpec(memory_space=pl.ANY),
                  pl.BlockSpec((1, tm, LANES), lambda ei, bi, ci: (bi, ci, 0)),
                  pl.BlockSpec((1, 1, d, f), wmap), pl.BlockSpec((1, 1, d, f), wmap),
                  pl.BlockSpec((1, 1, f, d), wmap)],
        out_specs=pl.BlockSpec((1, 1, tm, d), lambda ei, bi, ci: (bi, ei, ci, 0)),
        out_shape=jax.ShapeDtypeStruct((batch, experts, cap, d), BF16),
        scratch_shapes=[pltpu.VMEM((2, tm * pack_rows, LANES), jnp.int32), pltpu.SemaphoreType.DMA((2,))],
        compiler_params=_cparams("arbitrary", "arbitrary", "arbitrary"),
        name="experts",
    )(idx_tiles, idx_tiles, xp, slots, wg_bf, wu_bf, wd_bf)


COMBINE_SLAB = 64
POS_SPLIT_BITS = 6


def _combine_kernel(p0_ref, y_hbm, pos_ref, x_ref, g_ref, b_ref, spread_ref, xo_ref, xb_ref,
                    slab_ref, extra_ref, acc_ref, sem, sem_extra, *, alpha, experts, tm, ntiles, cap, slab):
    g = pl.program_id(0)
    nsteps = pl.num_programs(0)

    def first_slot(step, e):
        bb = step // ntiles
        p0 = p0_ref[(bb * (ntiles + 1) + step % ntiles) * experts + e]
        return bb, (p0 // SUBLANES) * SUBLANES

    def first_slab_copy(step, slot, e):
        bb, a0 = first_slot(step, e)
        return pltpu.make_async_copy(
            y_hbm.at[bb, e, pl.ds(pl.multiple_of(jnp.minimum(a0, cap - slab), SUBLANES), slab), :],
            slab_ref.at[slot, pl.ds(e * slab, slab), :], sem.at[slot])

    def fetch(step, slot):
        for e in range(experts):
            first_slab_copy(step, slot, e).start()

    @pl.when(g == 0)
    def _():
        fetch(0, 0)

    @pl.when(g + 1 < nsteps)
    def _():
        fetch(g + 1, (g + 1) % 2)

    cur = g % 2
    for e in range(experts):
        first_slab_copy(g, cur, e).wait()

    q = pos_ref[...] + 1
    q_hi = jnp.right_shift(q, POS_SPLIT_BITS).astype(F32).astype(BF16)
    q_lo = jnp.bitwise_and(q, (1 << POS_SPLIT_BITS) - 1).astype(F32).astype(BF16)
    qx = (jnp.dot(q_hi, spread_ref[...], preferred_element_type=F32) * float(1 << POS_SPLIT_BITS)
          + jnp.dot(q_lo, spread_ref[...], preferred_element_type=F32))
    col = lax.broadcasted_iota(jnp.int32, (1, experts * slab), 1)
    target = col % slab + 1
    for e in range(experts):
        _, a0 = first_slot(g, e)
        target = target + jnp.where(col // slab == e, jnp.minimum(a0, cap - slab), 0)
    onehot = jnp.where(qx == target.astype(F32), 1.0, 0.0).astype(BF16)
    acc_ref[...] = jnp.dot(onehot, slab_ref[cur], preferred_element_type=F32)

    slot_iota = lax.broadcasted_iota(jnp.int32, (1, slab), 1)
    for e in range(experts):
        bb, a0 = first_slot(g, e)
        p1 = p0_ref[(bb * (ntiles + 1) + g % ntiles + 1) * experts + e]
        nslab = (p1 - a0 + slab - 1) // slab
        pcol = pos_ref[:, e:e + 1]

        def more(w, carry, bb=bb, a0=a0, pcol=pcol, e=e):
            nominal = a0 + w * slab
            start = jnp.minimum(nominal, cap - slab)
            cp = pltpu.make_async_copy(y_hbm.at[bb, e, pl.ds(pl.multiple_of(start, SUBLANES), slab), :],
                                       extra_ref, sem_extra.at[0])
            cp.start()
            cp.wait()
            hit = (pcol == start + slot_iota) & (pcol >= nominal)
            acc_ref[...] += jnp.dot(jnp.where(hit, 1.0, 0.0).astype(BF16), extra_ref[...],
                                    preferred_element_type=F32)
            return carry

        lax.fori_loop(1, nslab, more, 0)

    x2 = _layer_norm_rows(alpha * x_ref[...] + acc_ref[...], g_ref[0], b_ref[0])
    xo_ref[...] = x2
    xb_ref[...] = x2.astype(BF16)


def _combine(p0_flat, y, pos, x, ln_g, ln_b, layer, *, alpha, batch, seq, cap, experts, tm):
    t, d = x.shape
    ntiles = seq // tm
    slab = min(COMBINE_SLAB, cap)
    assert cap >> POS_SPLIT_BITS < 256 and experts <= LANES
    lane_expert = jnp.arange(experts * slab, dtype=jnp.int32)[None, :] // slab
    spread = (lane_expert == jnp.arange(LANES, dtype=jnp.int32)[:, None]).astype(BF16)
    row = lambda i, p0: (i, 0)
    lay = lambda i, p0: (layer, 0, 0)
    return pl.pallas_call(
        functools.partial(_combine_kernel, alpha=alpha, experts=experts, tm=tm, ntiles=ntiles, cap=cap, slab=slab),
        grid_spec=pltpu.PrefetchScalarGridSpec(
            num_scalar_prefetch=1,
            grid=(t // tm,),
            in_specs=[pl.BlockSpec(memory_space=pl.ANY),
                      pl.BlockSpec((tm, LANES), row), pl.BlockSpec((tm, d), row),
                      pl.BlockSpec((1, 1, d), lay), pl.BlockSpec((1, 1, d), lay),
                      pl.BlockSpec((LANES, experts * slab), lambda i, p0: (0, 0))],
            out_specs=[pl.BlockSpec((tm, d), row), pl.BlockSpec((tm, d), row)],
            scratch_shapes=[pltpu.VMEM((2, experts * slab, d), BF16),
                            pltpu.VMEM((slab, d), BF16),
                            pltpu.VMEM((tm, d), F32),
                            pltpu.SemaphoreType.DMA((2,)),
                            pltpu.SemaphoreType.DMA((1,))]),
        out_shape=[jax.ShapeDtypeStruct((t, d), F32), jax.ShapeDtypeStruct((t, d), BF16)],
        compiler_params=_cparams("arbitrary"),
        name="combine_ln2",
    )(p0_flat, y, pos, x, ln_g, ln_b, spread)


def kernel(x, w_in, b_if, w_pool_grp, pool_scale, w_pool_up, mlstm_norm_w, w_mlstm_up, w_out, ln1_g, ln1_b,
           w_router, w_gate, w_up, w_down, ln2_g, ln2_b):
    batch, seq, d = x.shape
    depth = w_in.shape[0]
    t = batch * seq
    heads = MLSTM_HEADS
    pw = w_pool_up.shape[1]
    mw = w_mlstm_up.shape[1]
    dh = mw // heads
    experts = w_router.shape[2]
    cap = EC_CAPACITY_FACTOR * seq // experts
    ngates = 4 * heads
    alpha = (2.0 * depth) ** 0.25
    lk = _tile(seq, 256)
    rt = _tile(seq, ROUTE_TILE)
    et = _tile(cap, EXPERT_TILE)
    ref_q = pw
    ref_k = ref_q + mw
    ref_if = pw + 4 * mw
    ref_gp = ref_if + ngates
    col_q = 0
    col_k = col_q + mw
    col_v = col_k + mw
    col_o = col_v + mw
    col_gp = col_o + mw
    col_gm = col_gp + d
    col_pool = col_gm + d

    w_main = jnp.concatenate([w_in[:, :, ref_q:ref_k] * (dh ** -0.5), w_in[:, :, ref_k:ref_if],
                              w_in[:, :, ref_gp:], w_in[:, :, :ref_q]], axis=2).astype(BF16)
    w_gates = jnp.concatenate([w_in[:, :, ref_if:ref_gp], jnp.zeros((depth, d, LANES - ngates), F32)],
                              axis=2).astype(BF16)
    gate_bias = jnp.concatenate([b_if, jnp.zeros((depth, LANES - ngates), F32)], axis=1).reshape(depth, 1, LANES)
    w_router_bf = jnp.concatenate([w_router, jnp.zeros((depth, d, LANES - experts), F32)], axis=2).astype(BF16)
    w_pool_grp_bf = w_pool_grp.astype(BF16)
    w_pool_up_bf = w_pool_up.astype(BF16)
    w_mlstm_up_bf = w_mlstm_up.astype(BF16)
    w_out_bf = w_out.astype(BF16)
    w_gate_bf = w_gate.astype(BF16)
    w_up_bf = w_up.astype(BF16)
    w_down_bf = w_down.astype(BF16)
    row3 = lambda a: a.reshape(depth, 1, a.shape[-1])
    pool_scale3, norm_w3 = row3(pool_scale), row3(mlstm_norm_w)
    ln1_g3, ln1_b3, ln2_g3, ln2_b3 = row3(ln1_g), row3(ln1_b), row3(ln2_g), row3(ln2_b)
    batch_row0 = (jnp.arange(batch, dtype=jnp.int32) * seq)[None, :, None]

    xf = x.reshape(t, d)
    xb = xf.astype(BF16)
    for layer in range(depth):
        proj = _inproj(xb, w_main, layer, (col_gp, col_pool))
        pc, pr = _gateprep(xb, w_gates, gate_bias, layer, lk, heads)
        common = dict(batch=batch, seq=seq, lk=lk, heads=heads, dh=dh)
        hf = _mlstm(proj, pc, pr, col_q, col_k, col_v, col_o, None, None, layer, reverse=False, **common)
        m = _mlstm(proj, pc, pr, col_q, col_k, col_v, col_o, hf, norm_w3, layer, reverse=True, **common)
        p = _pool(proj, w_pool_grp_bf, pool_scale3, layer, seq=seq, width=pw, col_pool=col_pool)
        mixin = _merge(p, m, w_pool_up_bf, w_mlstm_up_bf, layer, proj, col_gp, col_gm)
        x1, x1p, aff = _outproj(mixin, w_out_bf, xf, ln1_g3, ln1_b3, w_router_bf, layer, alpha=alpha, experts=experts)

        pos, p0 = _route(aff, batch=batch, seq=seq, cap=cap, tm=rt)
        p0_flat = jnp.concatenate([p0[:, :, :experts], jnp.full((batch, 1, experts), cap, jnp.int32)],
                                  axis=1).reshape(-1)
        slots = _compact(p0_flat, pos, aff, batch=batch, seq=seq, cap=cap, experts=experts, tm=rt)
        tok = (slots[:, :cap, 0::SLOT_VALUES] * LANES + slots[:, :cap, 1::SLOT_VALUES])[..., :experts]
        idx_tiles = (tok.astype(jnp.int32).transpose(2, 0, 1) + batch_row0).reshape(-1, 1, et)
        y = _experts(idx_tiles, x1p, slots, w_gate_bf, w_up_bf, w_down_bf, layer,
                     batch=batch, experts=experts, cap=cap, d=d, tm=et)
        xf, xb = _combine(p0_flat, y, pos, x1, ln2_g3, ln2_b3, layer,
                          alpha=alpha, batch=batch, seq=seq, cap=cap, experts=experts, tm=rt)
    return xf.reshape(batch, seq, d)
```

```python
import functools

import jax
import jax.numpy as jnp
from jax import lax
from jax.experimental import pallas as pl
from jax.experimental.pallas import tpu as pltpu

MLSTM_HEADS = 8
POOL_WINDOWS = (2, 4, 8, 16)
POOL_HALO = max(POOL_WINDOWS) // 2
EC_CAPACITY_FACTOR = 2
LN_EPS = 1e-5
GN_EPS = 1e-6

LANES = 128
SUBLANES = 8
VMEM_LIMIT_BYTES = 56 * 1024 * 1024

BF16 = jnp.bfloat16
F32 = jnp.float32


def _cparams(*semantics):
    return pltpu.CompilerParams(dimension_semantics=semantics, vmem_limit_bytes=VMEM_LIMIT_BYTES)


def _tile(n, want):
    t = min(n, want)
    while n % t:
        t //= 2
    return t


def _sigmoid(x):
    return 0.5 * jnp.tanh(0.5 * x) + 0.5


INPROJ_TILE_M = 2048
INPROJ_TILE_N = 1024


def _inproj_kernel(x_ref, w_ref, o_ref, *, sigmoid):
    acc = jnp.dot(x_ref[...], w_ref[0], preferred_element_type=F32)
    o_ref[...] = (_sigmoid(acc) if sigmoid else acc).astype(o_ref.dtype)


def _inproj(x_bf, w_bf, layer, *, sigmoid):
    t, d = x_bf.shape
    n = w_bf.shape[2]
    tm = _tile(t, INPROJ_TILE_M)
    tn = _tile(n, INPROJ_TILE_N)
    return pl.pallas_call(
        functools.partial(_inproj_kernel, sigmoid=sigmoid),
        grid=(t // tm, n // tn),
        in_specs=[pl.BlockSpec((tm, d), lambda i, j: (i, 0)),
                  pl.BlockSpec((1, d, tn), lambda i, j: (layer, 0, j))],
        out_specs=pl.BlockSpec((tm, tn), lambda i, j: (i, j)),
        out_shape=jax.ShapeDtypeStruct((t, n), BF16),
        compiler_params=_cparams("parallel", "arbitrary"),
        name="inproj",
    )(x_bf, w_bf)


def _split3(a):
    hi = a.astype(BF16)
    r1 = a - hi.astype(F32)
    mid = r1.astype(BF16)
    lo = (r1 - mid.astype(F32)).astype(BF16)
    return hi, mid, lo


def _gateprep_kernel(x_ref, w_ref, b_ref, col_ref, row_ref, *, heads):
    lk = x_ref.shape[0]
    g = jnp.dot(x_ref[...], w_ref[0], preferred_element_type=F32) + b_ref[0]
    logf = jax.nn.log_sigmoid(g)
    r = lax.broadcasted_iota(jnp.int32, (lk, lk), 0)
    c = lax.broadcasted_iota(jnp.int32, (lk, lk), 1)
    lower = (c <= r).astype(BF16)
    upper = (c >= r).astype(BF16)
    parts = _split3(logf)
    pre = sum(jnp.dot(lower, p, preferred_element_type=F32) for p in parts)
    suf = sum(jnp.dot(upper, p, preferred_element_type=F32) for p in parts)
    lane = lax.broadcasted_iota(jnp.int32, g.shape, 1)
    in_group = lambda k: (lane >= k * heads) & (lane < (k + 1) * heads)
    col = jnp.where(in_group(1), pre, jnp.where(in_group(3), suf, g))
    diff = col - pltpu.roll(col, LANES - heads, 1)
    row_id = lax.broadcasted_iota(jnp.int32, g.shape, 0)
    run_fwd = diff
    run_bwd = diff
    k = 1
    while k < lk:
        run_fwd = jnp.maximum(run_fwd, jnp.where(row_id >= k, pltpu.roll(run_fwd, k, 0), -jnp.inf))
        run_bwd = jnp.maximum(run_bwd, jnp.where(row_id < lk - k, pltpu.roll(run_bwd, lk - k, 0), -jnp.inf))
        k *= 2
    rowmax_fwd = pltpu.roll(col, 3 * heads, 1) + pltpu.roll(run_fwd, 4 * heads, 1)
    rowmax_bwd = pltpu.roll(col, 2 * heads, 1) + pltpu.roll(run_bwd, 3 * heads, 1)
    col = jnp.where(in_group(4), rowmax_fwd, jnp.where(in_group(5), rowmax_bwd, col))
    col_ref[...] = col
    row_ref[0] = col.T[: 4 * heads, :]


def _gateprep(x_bf, wg_bf, bias, layer, lk, heads):
    t, d = x_bf.shape
    nchunks = t // lk
    return pl.pallas_call(
        functools.partial(_gateprep_kernel, heads=heads),
        grid=(nchunks,),
        in_specs=[pl.BlockSpec((lk, d), lambda i: (i, 0)),
                  pl.BlockSpec((1, d, LANES), lambda i: (layer, 0, 0)),
                  pl.BlockSpec((1, 1, LANES), lambda i: (layer, 0, 0))],
        out_specs=[pl.BlockSpec((lk, LANES), lambda i: (i, 0)),
                   pl.BlockSpec((1, 4 * heads, lk), lambda i: (i, 0, 0))],
        out_shape=[jax.ShapeDtypeStruct((t, LANES), F32),
                   jax.ShapeDtypeStruct((nchunks, 4 * heads, lk), F32)],
        compiler_params=_cparams("parallel"),
        name="gateprep",
    )(x_bf, wg_bf, bias)


def _mlstm_chunk(q, k, v, i_col, b_col, rowmax_col, i_row, b_row, ct, n, m_prev, *, reverse):
    lk = q.shape[0]
    row = lax.broadcasted_iota(jnp.int32, (lk, lk), 0)
    col = lax.broadcasted_iota(jnp.int32, (lk, lk), 1)
    mask = (col >= row) if reverse else (col <= row)
    inter = b_col + m_prev
    m_t = jnp.maximum(inter, rowmax_col)
    w_inter = jnp.exp(inter - m_t)
    gates = jnp.where(mask, jnp.exp((b_col - m_t) + (i_row - b_row)), 0.0)
    a = gates * lax.dot_general(q, k, (((1,), (1,)), ((), ())), preferred_element_type=F32)
    num = (jnp.dot(a.astype(BF16), v, preferred_element_type=F32)
           + w_inter * jnp.dot(q, ct.astype(BF16), preferred_element_type=F32))
    qn = jnp.sum(q.astype(F32) * n.astype(BF16).astype(F32), axis=1, keepdims=True)
    den = jnp.sum(a, axis=1, keepdims=True) + w_inter * qn
    h = num * (1.0 / jnp.maximum(jnp.abs(den), jnp.exp(-m_t)))
    g = b_col[0:1, :] if reverse else b_col[lk - 1:lk, :]
    a_col = g - b_col + i_col
    m_new = jnp.maximum(g + m_prev, jnp.max(a_col, axis=0, keepdims=True))
    decay = jnp.exp(g + m_prev - m_new)
    wa = jnp.exp(a_col - m_new)
    vw = (v.astype(F32) * wa).astype(BF16)
    ct_new = decay * ct + lax.dot_general(k, vw, (((0,), (0,)), ((), ())), preferred_element_type=F32)
    n_new = decay * n + jnp.sum(wa.astype(BF16).astype(F32) * k.astype(F32), axis=0, keepdims=True)
    return h, ct_new, n_new, m_new


def _mlstm_kernel(*refs, heads, dh, reverse):
    if reverse:
        (q_ref, k_ref, v_ref, pc_ref, pr_ref, hf_ref, o_ref, nw_ref, out_ref, ct_ref, n_ref, m_ref) = refs
    else:
        (q_ref, k_ref, v_ref, pc_ref, pr_ref, out_ref, ct_ref, n_ref, m_ref) = refs

    @pl.when(pl.program_id(1) == 0)
    def _():
        ct_ref[...] = jnp.zeros_like(ct_ref)
        n_ref[...] = jnp.zeros_like(n_ref)
        m_ref[...] = jnp.zeros_like(m_ref)

    ig = 2 * heads if reverse else 0
    fg = ig + heads
    mg = (5 if reverse else 4) * heads
    pc = pc_ref[...]
    pr = pr_ref[0]
    for h in range(heads):
        sl = slice(h * dh, (h + 1) * dh)
        hh, ct_new, n_new, m_new = _mlstm_chunk(
            q_ref[:, sl], k_ref[:, sl], v_ref[:, sl],
            pc[:, ig + h:ig + h + 1], pc[:, fg + h:fg + h + 1], pc[:, mg + h:mg + h + 1],
            pr[ig + h:ig + h + 1, :], pr[fg + h:fg + h + 1, :],
            ct_ref[h], n_ref[h], m_ref[h], reverse=reverse)
        ct_ref[h] = ct_new
        n_ref[h] = n_new
        m_ref[h] = m_new
        if reverse:
            tot = hf_ref[:, sl] + hh
            mu = jnp.mean(tot, axis=1, keepdims=True)
            cen = tot - mu
            var = jnp.mean(cen * cen, axis=1, keepdims=True)
            hn = cen * lax.rsqrt(var + GN_EPS) * nw_ref[0, :, sl]
            out_ref[:, sl] = (_sigmoid(o_ref[:, sl].astype(F32)) * hn).astype(out_ref.dtype)
        else:
            out_ref[:, sl] = hh


def _mlstm(proj, pc, pr, col_q, col_k, col_v, col_o, hf, norm_w, layer, *, batch, seq, lk, heads, dh, reverse):
    t = batch * seq
    width = heads * dh
    nc = seq // lk
    chunk = (lambda b, c: b * nc + (nc - 1 - c)) if reverse else (lambda b, c: b * nc + c)
    tok_spec = lambda colblk: pl.BlockSpec((lk, width), lambda b, c: (chunk(b, c), colblk))
    in_specs = [tok_spec(col_q // width), tok_spec(col_k // width), tok_spec(col_v // width),
                pl.BlockSpec((lk, LANES), lambda b, c: (chunk(b, c), 0)),
                pl.BlockSpec((1, 4 * heads, lk), lambda b, c: (chunk(b, c), 0, 0))]
    args = [proj, proj, proj, pc, pr]
    if reverse:
        in_specs += [pl.BlockSpec((lk, width), lambda b, c: (chunk(b, c), 0)),
                     tok_spec(col_o // width),
                     pl.BlockSpec((1, 1, width), lambda b, c: (layer, 0, 0))]
        args += [hf, proj, norm_w]
        out_dtype = BF16
    else:
        out_dtype = F32
    return pl.pallas_call(
        functools.partial(_mlstm_kernel, heads=heads, dh=dh, reverse=reverse),
        grid=(batch, nc),
        in_specs=in_specs,
        out_specs=pl.BlockSpec((lk, width), lambda b, c: (chunk(b, c), 0)),
        out_shape=jax.ShapeDtypeStruct((t, width), out_dtype),
        scratch_shapes=[pltpu.VMEM((heads, dh, dh), F32),
                        pltpu.VMEM((heads, 1, dh), F32),
                        pltpu.VMEM((heads, 1, 1), F32)],
        compiler_params=_cparams("parallel", "arbitrary"),
        name="mlstm_bwd" if reverse else "mlstm_fwd",
    )(*args)


def _pool_kernel(prev_ref, cur_ref, next_ref, w_ref, s_ref, o_ref, *, seq, gd):
    tm = cur_ref.shape[0]
    i = pl.program_id(0)
    pos0 = (i * tm) % seq
    at_start = pos0 == 0
    at_end = pos0 + tm == seq
    prev = jnp.where(at_start, 0.0, prev_ref[...].astype(F32))
    nxt = jnp.where(at_end, 0.0, next_ref[...].astype(F32))
    cur = cur_ref[...].astype(F32)
    ext = jnp.concatenate([prev, cur, nxt], axis=0)
    n_ext = ext.shape[0]
    pos = pos0 + lax.broadcasted_iota(jnp.int32, (tm, 1), 0)
    for grp, win in enumerate(POOL_WINDOWS):
        e = ext[:, grp * gd:(grp + 1) * gd]
        acc = e + pltpu.roll(e, 1, 0)
        half = 1
        while 2 * half < win:
            acc = pltpu.roll(acc, n_ext - half, 0) + pltpu.roll(acc, half, 0)
            half *= 2
        wsum = acc[POOL_HALO:POOL_HALO + tm]
        lo = jnp.maximum(pos - win // 2, 0)
        hi = jnp.minimum(pos + win // 2, seq)
        cnt = (hi - lo).astype(F32)
        diff = wsum / cnt - cur[:, grp * gd:(grp + 1) * gd]
        y = jnp.dot(diff.astype(BF16), w_ref[0, grp], preferred_element_type=F32)
        o_ref[:, grp * gd:(grp + 1) * gd] = (y * s_ref[0, :, grp * gd:(grp + 1) * gd]).astype(o_ref.dtype)


def _pool(proj, w_grp_bf, scale, layer, *, seq, width, col_pool):
    t = proj.shape[0]
    _, ngrp, gd, _ = w_grp_bf.shape
    tm = _tile(seq, 512)
    hb = tm // POOL_HALO
    nhalo = t // POOL_HALO
    cb = col_pool // width
    return pl.pallas_call(
        functools.partial(_pool_kernel, seq=seq, gd=gd),
        grid=(t // tm,),
        in_specs=[pl.BlockSpec((POOL_HALO, width), lambda i: (jnp.maximum(i * hb - 1, 0), cb)),
                  pl.BlockSpec((tm, width), lambda i: (i, cb)),
                  pl.BlockSpec((POOL_HALO, width), lambda i: (jnp.minimum((i + 1) * hb, nhalo - 1), cb)),
                  pl.BlockSpec((1, ngrp, gd, gd), lambda i: (layer, 0, 0, 0)),
                  pl.BlockSpec((1, 1, width), lambda i: (layer, 0, 0))],
        out_specs=pl.BlockSpec((tm, width), lambda i: (i, 0)),
        out_shape=jax.ShapeDtypeStruct((t, width), BF16),
        compiler_params=_cparams("parallel"),
        name="pool",
    )(proj, proj, proj, w_grp_bf, scale)


def _merge_kernel(p_ref, m_ref, wp_ref, wm_ref, gp_ref, gm_ref, o_ref):
    yp = jnp.dot(p_ref[...], wp_ref[0], preferred_element_type=F32)
    ym = jnp.dot(m_ref[...], wm_ref[0], preferred_element_type=F32)
    o_ref[...] = (gp_ref[...].astype(F32) * yp + gm_ref[...].astype(F32) * ym).astype(o_ref.dtype)


def _merge(p, m, wp_bf, wm_bf, layer, proj, col_gp, col_gm):
    t, pw = p.shape
    mw = m.shape[1]
    d = wp_bf.shape[2]
    tm = _tile(t, 1024)
    tn = _tile(d, 1024)
    return pl.pallas_call(
        _merge_kernel,
        grid=(t // tm, d // tn),
        in_specs=[pl.BlockSpec((tm, pw), lambda i, j: (i, 0)),
                  pl.BlockSpec((tm, mw), lambda i, j: (i, 0)),
                  pl.BlockSpec((1, pw, tn), lambda i, j: (layer, 0, j)),
                  pl.BlockSpec((1, mw, tn), lambda i, j: (layer, 0, j)),
                  pl.BlockSpec((tm, tn), lambda i, j: (i, col_gp // tn + j)),
                  pl.BlockSpec((tm, tn), lambda i, j: (i, col_gm // tn + j))],
        out_specs=pl.BlockSpec((tm, tn), lambda i, j: (i, j)),
        out_shape=jax.ShapeDtypeStruct((t, d), BF16),
        compiler_params=_cparams("parallel", "arbitrary"),
        name="merge",
    )(p, m, wp_bf, wm_bf, proj, proj)


def _layer_norm_rows(y, g, b):
    mu = jnp.mean(y, axis=1, keepdims=True)
    cen = y - mu
    var = jnp.mean(cen * cen, axis=1, keepdims=True)
    return cen * lax.rsqrt(var + LN_EPS) * g + b


def _pack_rows(x_bf, xp_ref):
    tm, d = x_bf.shape
    half = d // 2
    nchunks = half // LANES
    bits = lax.bitcast_convert_type(x_bf.astype(F32), jnp.uint32)
    for j in range(nchunks):
        lo = jnp.right_shift(bits[:, j * LANES:(j + 1) * LANES], jnp.uint32(16))
        hi = jnp.bitwise_and(bits[:, half + j * LANES:half + (j + 1) * LANES], jnp.uint32(0xFFFF0000))
        xp_ref[pl.ds(j, tm, stride=nchunks), :] = lax.bitcast_convert_type(lo | hi, jnp.int32)


def _unpack_rows(buf_ref, tm, nchunks):
    lo, hi = [], []
    for j in range(nchunks):
        w = buf_ref[pl.ds(j, tm, stride=nchunks), :]
        lo.append(lax.bitcast_convert_type(jnp.left_shift(w, 16), F32).astype(BF16))
        hi.append(lax.bitcast_convert_type(jnp.bitwise_and(w, jnp.int32(-65536)), F32).astype(BF16))
    return jnp.concatenate(lo + hi, axis=1)


def _outproj_kernel(mix_ref, w_ref, x_ref, g_ref, b_ref, wr_ref, xo_ref, xp_ref, aff_ref, *, alpha, experts):
    mix = jnp.dot(mix_ref[...], w_ref[0], preferred_element_type=F32)
    x1 = _layer_norm_rows(alpha * x_ref[...] + mix, g_ref[0], b_ref[0])
    xo_ref[...] = x1
    x1b = x1.astype(BF16)
    _pack_rows(x1b, xp_ref)
    logits = jnp.dot(x1b, wr_ref[0], preferred_element_type=F32)
    lane = lax.broadcasted_iota(jnp.int32, logits.shape, 1)
    logits = jnp.where(lane < experts, logits, -jnp.inf)
    e = jnp.exp(logits - jnp.max(logits, axis=1, keepdims=True))
    aff_ref[...] = e / jnp.sum(e, axis=1, keepdims=True)


def _outproj(mixin, wout_bf, x, ln_g, ln_b, wr_bf, layer, *, alpha, experts):
    t, d = x.shape
    tm = _tile(t, 256)
    pack_rows = d // (2 * LANES)
    assert pack_rows == SUBLANES, "one (8,128) int32 tile per packed token row"
    row = lambda i: (i, 0)
    lay = lambda i: (layer, 0, 0)
    return pl.pallas_call(
        functools.partial(_outproj_kernel, alpha=alpha, experts=experts),
        grid=(t // tm,),
        in_specs=[pl.BlockSpec((tm, d), row), pl.BlockSpec((1, d, d), lay), pl.BlockSpec((tm, d), row),
                  pl.BlockSpec((1, 1, d), lay), pl.BlockSpec((1, 1, d), lay), pl.BlockSpec((1, d, LANES), lay)],
        out_specs=[pl.BlockSpec((tm, d), row), pl.BlockSpec((tm * pack_rows, LANES), row),
                   pl.BlockSpec((tm, LANES), row)],
        out_shape=[jax.ShapeDtypeStruct((t, d), F32), jax.ShapeDtypeStruct((t * pack_rows, LANES), jnp.int32),
                   jax.ShapeDtypeStruct((t, LANES), F32)],
        compiler_params=_cparams("parallel"),
        name="outproj_ln",
    )(mixin, wout_bf, x, ln_g, ln_b, wr_bf)


ROUTE_TILE = 256
ROUTE_COUNT_CHUNK = 1024


def _route_kernel(aff_ref, pos_ref, p0_ref, *, cap, tm):
    s = aff_ref.shape[0]
    chunk = min(ROUTE_COUNT_CHUNK, s)

    def count_ge(cand):
        def body(c, acc):
            blk = aff_ref[pl.ds(pl.multiple_of(c * chunk, chunk), chunk), :]
            return acc + jnp.sum(jnp.where(blk >= cand, 1, 0), axis=0, keepdims=True)
        return lax.fori_loop(0, s // chunk, body, jnp.zeros((1, LANES), jnp.int32))

    def bit_body(k, thr_bits):
        cand = thr_bits | jnp.left_shift(jnp.int32(1), 30 - k)
        cnt = count_ge(lax.bitcast_convert_type(cand, F32))
        return jnp.where(cnt >= cap, cand, thr_bits)

    thr_bits = lax.fori_loop(0, 31, bit_body, jnp.zeros((1, LANES), jnp.int32))
    thr = lax.bitcast_convert_type(thr_bits, F32)
    n_gt = count_ge(lax.bitcast_convert_type(thr_bits + 1, F32))
    need = (cap - n_gt).astype(F32)

    r = lax.broadcasted_iota(jnp.int32, (tm, tm), 0)
    c = lax.broadcasted_iota(jnp.int32, (tm, tm), 1)
    tri = jnp.where(c < r, 1.0, 0.0).astype(BF16)

    def tile_body(j, carry):
        run_eq, run_sel = carry
        rows = pl.ds(pl.multiple_of(j * tm, tm), tm)
        blk = aff_ref[rows, :]
        gt = blk > thr
        eq = blk == thr
        eqf = jnp.where(eq, 1.0, 0.0)
        eq_rank = run_eq + jnp.dot(tri, eqf.astype(BF16), preferred_element_type=F32)
        sel = gt | (eq & (eq_rank < need))
        self_ = jnp.where(sel, 1.0, 0.0)
        pos = run_sel + jnp.dot(tri, self_.astype(BF16), preferred_element_type=F32)
        pos_ref[rows, :] = jnp.where(sel, pos, -1.0).astype(jnp.int32)
        p0_ref[0, pl.ds(j, 1), :] = run_sel.astype(jnp.int32)
        return (run_eq + jnp.sum(eqf, axis=0, keepdims=True), run_sel + jnp.sum(self_, axis=0, keepdims=True))

    zero = jnp.zeros((1, LANES), F32)
    lax.fori_loop(0, s // tm, tile_body, (zero, zero))


def _route(aff, *, batch, seq, cap, tm):
    t = aff.shape[0]
    ntiles = seq // tm
    return pl.pallas_call(
        functools.partial(_route_kernel, cap=cap, tm=tm),
        grid=(batch,),
        in_specs=[pl.BlockSpec((seq, LANES), lambda b: (b, 0))],
        out_specs=[pl.BlockSpec((seq, LANES), lambda b: (b, 0)),
                   pl.BlockSpec((1, ntiles, LANES), lambda b: (b, 0, 0))],
        out_shape=[jax.ShapeDtypeStruct((t, LANES), jnp.int32),
                   jax.ShapeDtypeStruct((batch, ntiles, LANES), jnp.int32)],
        compiler_params=_cparams("parallel"),
        name="route",
    )(aff)


SLOT_VALUES = 8
COMPACT_WINDOW = 128


def _compact_kernel(p0_ref, pos_ref, aff_ref, out_ref, *, experts, tm, ntiles, win):
    b = pl.program_id(0)
    j = pl.program_id(1)

    @pl.when(j == 0)
    def _():
        out_ref[...] = jnp.zeros_like(out_ref)

    post = pos_ref[...].astype(F32).T
    hi, mid, lo = _split3(aff_ref[...])
    rr = lax.broadcasted_iota(jnp.int32, (LANES, LANES), 0)
    cc = lax.broadcasted_iota(jnp.int32, (LANES, LANES), 1)
    spread = lambda k: jnp.where((cc == rr * SLOT_VALUES + k) & (rr < experts), 1.0, 0.0).astype(BF16)
    vals = (jnp.dot(hi, spread(2), preferred_element_type=F32)
            + jnp.dot(mid, spread(3), preferred_element_type=F32)
            + jnp.dot(lo, spread(4), preferred_element_type=F32))
    tok = j * tm + lax.broadcasted_iota(jnp.int32, (tm, LANES), 0)
    lane = lax.broadcasted_iota(jnp.int32, (tm, LANES), 1)
    k8 = lane % SLOT_VALUES
    vals = jnp.where(k8 == 0, (tok // LANES).astype(F32), jnp.where(k8 == 1, (tok % LANES).astype(F32), vals))
    vals = vals.astype(BF16)
    lane_expert = lane // SLOT_VALUES
    slot_iota = lax.broadcasted_iota(jnp.int32, (win, 1), 0)
    crowded = []
    for e in range(experts):
        base = (b * (ntiles + 1) + j) * experts + e
        p0 = p0_ref[base]
        p1 = p0_ref[base + experts]
        a0 = (p0 // SUBLANES) * SUBLANES
        nwin = (p1 - a0 + win - 1) // win
        ve = jnp.where(lane_expert == e, vals, jnp.zeros_like(vals))
        prow = post[e:e + 1, :]

        def add_window(w, carry, a0=a0, ve=ve, prow=prow):
            start = pl.multiple_of(a0 + w * win, SUBLANES)
            onehot = jnp.where(prow == (start + slot_iota).astype(F32), 1.0, 0.0).astype(BF16)
            out_ref[0, pl.ds(start, win), :] += jnp.dot(onehot, ve, preferred_element_type=F32)
            return carry

        add_window(0, 0)
        crowded.append((nwin, add_window))

    for nwin, add_window in crowded:
        lax.fori_loop(1, nwin, add_window, 0)


def _compact(p0_flat, pos, aff, *, batch, seq, cap, experts, tm):
    ntiles = seq // tm
    win = COMPACT_WINDOW
    cpad = cap + win
    tok = lambda b, j, p0: (b * ntiles + j, 0)
    return pl.pallas_call(
        functools.partial(_compact_kernel, experts=experts, tm=tm, ntiles=ntiles, win=win),
        grid_spec=pltpu.PrefetchScalarGridSpec(
            num_scalar_prefetch=1,
            grid=(batch, ntiles),
            in_specs=[pl.BlockSpec((tm, LANES), tok), pl.BlockSpec((tm, LANES), tok)],
            out_specs=pl.BlockSpec((1, cpad, LANES), lambda b, j, p0: (b, 0, 0))),
        out_shape=jax.ShapeDtypeStruct((batch, cpad, LANES), F32),
        compiler_params=_cparams("parallel", "arbitrary"),
        name="compact",
    )(p0_flat, pos, aff)


EXPERT_TILE = 512


def _expert_kernel(idx_ref, idx_next_ref, xp_hbm, slots_ref, wg_ref, wu_ref, wd_ref, y_ref, buf_ref, sem,
                   *, tm, pack_rows):
    e = pl.program_id(0)
    step = (e * pl.num_programs(1) + pl.program_id(1)) * pl.num_programs(2) + pl.program_id(2)
    last = pl.num_programs(0) * pl.num_programs(1) * pl.num_programs(2) - 1

    def start_rows(ids_ref, slot):
        for i in range(tm):
            src = xp_hbm.at[pl.ds(pl.multiple_of(ids_ref[0, 0, i] * pack_rows, pack_rows), pack_rows), :]
            pltpu.make_async_copy(src, buf_ref.at[slot, pl.ds(i * pack_rows, pack_rows), :], sem.at[slot]).start()

    def wait_rows(slot):
        pltpu.make_async_copy(xp_hbm.at[pl.ds(0, tm * pack_rows), :], buf_ref.at[slot], sem.at[slot]).wait()

    @pl.when(step == 0)
    def _():
        start_rows(idx_ref, 0)

    cur = step % 2
    start_rows(idx_next_ref, 1 - cur)
    wait_rows(cur)

    xe = _unpack_rows(buf_ref.at[cur], tm, pack_rows)
    g = jnp.dot(xe, wg_ref[0, 0], preferred_element_type=F32)
    u = jnp.dot(xe, wu_ref[0, 0], preferred_element_type=F32)
    hid = (g * _sigmoid(g) * u).astype(BF16)
    slots = slots_ref[0]
    lane = lax.broadcasted_iota(jnp.int32, slots.shape, 1)
    is_gate = (lane // SLOT_VALUES == e) & (lane % SLOT_VALUES >= 2) & (lane % SLOT_VALUES <= 4)
    gate = jnp.sum(jnp.where(is_gate, slots, 0.0), axis=1, keepdims=True)
    y = jnp.dot(hid, wd_ref[0, 0], preferred_element_type=F32) * gate
    y_ref[0, 0] = y.astype(y_ref.dtype)

    @pl.when(step == last)
    def _():
        wait_rows(1 - cur)


def _experts(idx_tiles, xp, slots, wg_bf, wu_bf, wd_bf, layer, *, batch, experts, cap, d, tm):
    f = wg_bf.shape[3]
    nct = cap // tm
    nsteps = experts * batch * nct
    pack_rows = d // (2 * LANES)
    step = lambda ei, bi, ci: (ei * batch + bi) * nct + ci
    wmap = lambda ei, bi, ci: (layer, ei, 0, 0)
    return pl.pallas_call(
        functools.partial(_expert_kernel, tm=tm, pack_rows=pack_rows),
        grid=(experts, batch, nct),
        in_specs=[pl.BlockSpec((1, 1, tm), lambda ei, bi, ci: (step(ei, bi, ci), 0, 0), memory_space=pltpu.SMEM),
                  pl.BlockSpec((1, 1, tm), lambda ei, bi, ci: (jnp.minimum(step(ei, bi, ci) + 1, nsteps - 1), 0, 0),
                               memory_space=pltpu.SMEM),
                  pl.BlockSpec(memory_space=pl.ANY),
                  pl.BlockSpec((1, tm, LANES), lambda ei, bi, ci: (bi, ci, 0)),
                  pl.BlockSpec((1, 1, d, f), wmap), pl.BlockSpec((1, 1, d, f), wmap),
                  pl.BlockSpec((1, 1, f, d), wmap)],
        out_specs=pl.BlockSpec((1, 1, tm, d), lambda ei, bi, ci: (bi, ei, ci, 0)),
        out_shape=jax.ShapeDtypeStruct((batch, experts, cap, d), BF16),
        scratch_shapes=[pltpu.VMEM((2, tm * pack_rows, LANES), jnp.int32), pltpu.SemaphoreType.DMA((2,))],
        compiler_params=_cparams("arbitrary", "arbitrary", "arbitrary"),
        name="experts",
    )(idx_tiles, idx_tiles, xp, slots, wg_bf, wu_bf, wd_bf)


COMBINE_SLAB = 64
POS_SPLIT_BITS = 6


def _combine_kernel(p0_ref, y_hbm, pos_ref, x_ref, g_ref, b_ref, spread_ref, xo_ref, xb_ref,
                    slab_ref, extra_ref, acc_ref, sem, sem_extra, *, alpha, experts, tm, ntiles, cap, slab):
    g = pl.program_id(0)
    nsteps = pl.num_programs(0)

    def first_slot(step, e):
        bb = step // ntiles
        p0 = p0_ref[(bb * (ntiles + 1) + step % ntiles) * experts + e]
        return bb, (p0 // SUBLANES) * SUBLANES

    def first_slab_copy(step, slot, e):
        bb, a0 = first_slot(step, e)
        return pltpu.make_async_copy(
            y_hbm.at[bb, e, pl.ds(pl.multiple_of(jnp.minimum(a0, cap - slab), SUBLANES), slab), :],
            slab_ref.at[slot, pl.ds(e * slab, slab), :], sem.at[slot])

    def fetch(step, slot):
        for e in range(experts):
            first_slab_copy(step, slot, e).start()

    @pl.when(g == 0)
    def _():
        fetch(0, 0)

    @pl.when(g + 1 < nsteps)
    def _():
        fetch(g + 1, (g + 1) % 2)

    cur = g % 2
    for e in range(experts):
        first_slab_copy(g, cur, e).wait()

    q = pos_ref[...] + 1
    q_hi = jnp.right_shift(q, POS_SPLIT_BITS).astype(F32).astype(BF16)
    q_lo = jnp.bitwise_and(q, (1 << POS_SPLIT_BITS) - 1).astype(F32).astype(BF16)
    qx = (jnp.dot(q_hi, spread_ref[...], preferred_element_type=F32) * float(1 << POS_SPLIT_BITS)
          + jnp.dot(q_lo, spread_ref[...], preferred_element_type=F32))
    col = lax.broadcasted_iota(jnp.int32, (1, experts * slab), 1)
    target = col % slab + 1
    for e in range(experts):
        _, a0 = first_slot(g, e)
        target = target + jnp.where(col // slab == e, jnp.minimum(a0, cap - slab), 0)
    onehot = jnp.where(qx == target.astype(F32), 1.0, 0.0).astype(BF16)
    acc_ref[...] = jnp.dot(onehot, slab_ref[cur], preferred_element_type=F32)

    slot_iota = lax.broadcasted_iota(jnp.int32, (1, slab), 1)
    for e in range(experts):
        bb, a0 = first_slot(g, e)
        p1 = p0_ref[(bb * (ntiles + 1) + g % ntiles + 1) * experts + e]
        nslab = (p1 - a0 + slab - 1) // slab
        pcol = pos_ref[:, e:e + 1]

        def more(w, carry, bb=bb, a0=a0, pcol=pcol, e=e):
            nominal = a0 + w * slab
            start = jnp.minimum(nominal, cap - slab)
            cp = pltpu.make_async_copy(y_hbm.at[bb, e, pl.ds(pl.multiple_of(start, SUBLANES), slab), :],
                                       extra_ref, sem_extra.at[0])
            cp.start()
            cp.wait()
            hit = (pcol == start + slot_iota) & (pcol >= nominal)
            acc_ref[...] += jnp.dot(jnp.where(hit, 1.0, 0.0).astype(BF16), extra_ref[...],
                                    preferred_element_type=F32)
            return carry

        lax.fori_loop(1, nslab, more, 0)

    x2 = _layer_norm_rows(alpha * x_ref[...] + acc_ref[...], g_ref[0], b_ref[0])
    xo_ref[...] = x2
    xb_ref[...] = x2.astype(BF16)


def _combine(p0_flat, y, pos, x, ln_g, ln_b, layer, *, alpha, batch, seq, cap, experts, tm):
    t, d = x.shape
    ntiles = seq // tm
    slab = min(COMBINE_SLAB, cap)
    assert cap >> POS_SPLIT_BITS < 256 and experts <= LANES
    lane_expert = jnp.arange(experts * slab, dtype=jnp.int32)[None, :] // slab
    spread = (lane_expert == jnp.arange(LANES, dtype=jnp.int32)[:, None]).astype(BF16)
    row = lambda i, p0: (i, 0)
    lay = lambda i, p0: (layer, 0, 0)
    return pl.pallas_call(
        functools.partial(_combine_kernel, alpha=alpha, experts=experts, tm=tm, ntiles=ntiles, cap=cap, slab=slab),
        grid_spec=pltpu.PrefetchScalarGridSpec(
            num_scalar_prefetch=1,
            grid=(t // tm,),
            in_specs=[pl.BlockSpec(memory_space=pl.ANY),
                      pl.BlockSpec((tm, LANES), row), pl.BlockSpec((tm, d), row),
                      pl.BlockSpec((1, 1, d), lay), pl.BlockSpec((1, 1, d), lay),
                      pl.BlockSpec((LANES, experts * slab), lambda i, p0: (0, 0))],
            out_specs=[pl.BlockSpec((tm, d), row), pl.BlockSpec((tm, d), row)],
            scratch_shapes=[pltpu.VMEM((2, experts * slab, d), BF16),
                            pltpu.VMEM((slab, d), BF16),
                            pltpu.VMEM((tm, d), F32),
                            pltpu.SemaphoreType.DMA((2,)),
                            pltpu.SemaphoreType.DMA((1,))]),
        out_shape=[jax.ShapeDtypeStruct((t, d), F32), jax.ShapeDtypeStruct((t, d), BF16)],
        compiler_params=_cparams("arbitrary"),
        name="combine_ln2",
    )(p0_flat, y, pos, x, ln_g, ln_b, spread)


def kernel(x, w_in, b_if, w_pool_grp, pool_scale, w_pool_up, mlstm_norm_w, w_mlstm_up, w_out, ln1_g, ln1_b,
           w_router, w_gate, w_up, w_down, ln2_g, ln2_b):
    batch, seq, d = x.shape
    depth = w_in.shape[0]
    t = batch * seq
    heads = MLSTM_HEADS
    pw = w_pool_up.shape[1]
    mw = w_mlstm_up.shape[1]
    dh = mw // heads
    experts = w_router.shape[2]
    cap = EC_CAPACITY_FACTOR * seq // experts
    ngates = 4 * heads
    alpha = (2.0 * depth) ** 0.25
    lk = _tile(seq, 256)
    rt = _tile(seq, ROUTE_TILE)
    et = _tile(cap, EXPERT_TILE)
    ref_q = pw
    ref_k = ref_q + mw
    ref_if = pw + 4 * mw
    ref_gp = ref_if + ngates
    col_q, col_k, col_v, col_o = 0, mw, 2 * mw, 3 * mw
    col_gp, col_gm = 0, d

    q_scale = jnp.where(jnp.arange(4 * mw) < mw, dh ** -0.5, 1.0).astype(F32)
    w_qkvo = (w_in[:, :, ref_q:ref_if] * q_scale).astype(BF16)
    w_branch_gates = w_in[:, :, ref_gp:].astype(BF16)
    w_pool_in = w_in[:, :, :ref_q].astype(BF16)
    w_gates = jnp.concatenate([w_in[:, :, ref_if:ref_gp], jnp.zeros((depth, d, LANES - ngates), F32)],
                              axis=2).astype(BF16)
    gate_bias = jnp.concatenate([b_if, jnp.zeros((depth, LANES - ngates), F32)], axis=1).reshape(depth, 1, LANES)
    w_router_bf = jnp.concatenate([w_router, jnp.zeros((depth, d, LANES - experts), F32)], axis=2).astype(BF16)
    w_pool_grp_bf = w_pool_grp.astype(BF16)
    w_pool_up_bf = w_pool_up.astype(BF16)
    w_mlstm_up_bf = w_mlstm_up.astype(BF16)
    w_out_bf = w_out.astype(BF16)
    w_gate_bf = w_gate.astype(BF16)
    w_up_bf = w_up.astype(BF16)
    w_down_bf = w_down.astype(BF16)
    row3 = lambda a: a.reshape(depth, 1, a.shape[-1])
    pool_scale3, norm_w3 = row3(pool_scale), row3(mlstm_norm_w)
    ln1_g3, ln1_b3, ln2_g3, ln2_b3 = row3(ln1_g), row3(ln1_b), row3(ln2_g), row3(ln2_b)
    batch_row0 = (jnp.arange(batch, dtype=jnp.int32) * seq)[None, :, None]

    xf = x.reshape(t, d)
    xb = xf.astype(BF16)
    for layer in range(depth):
        proj = _inproj(xb, w_qkvo, layer, sigmoid=False)
        branch_gates = _inproj(xb, w_branch_gates, layer, sigmoid=True)
        pool_in = _inproj(xb, w_pool_in, layer, sigmoid=False)
        pc, pr = _gateprep(xb, w_gates, gate_bias, layer, lk, heads)
        common = dict(batch=batch, seq=seq, lk=lk, heads=heads, dh=dh)
        hf = _mlstm(proj, pc, pr, col_q, col_k, col_v, col_o, None, None, layer, reverse=False, **common)
        m = _mlstm(proj, pc, pr, col_q, col_k, col_v, col_o, hf, norm_w3, layer, reverse=True, **common)
        p = _pool(pool_in, w_pool_grp_bf, pool_scale3, layer, seq=seq, width=pw, col_pool=0)
        mixin = _merge(p, m, w_pool_up_bf, w_mlstm_up_bf, layer, branch_gates, col_gp, col_gm)
        x1, x1p, aff = _outproj(mixin, w_out_bf, xf, ln1_g3, ln1_b3, w_router_bf, layer, alpha=alpha, experts=experts)

        pos, p0 = _route(aff, batch=batch, seq=seq, cap=cap, tm=rt)
        p0_flat = jnp.concatenate([p0[:, :, :experts], jnp.full((batch, 1, experts), cap, jnp.int32)],
                                  axis=1).reshape(-1)
        slots = _compact(p0_flat, pos, aff, batch=batch, seq=seq, cap=cap, experts=experts, tm=rt)
        tok = (slots[:, :cap, 0::SLOT_VALUES] * LANES + slots[:, :cap, 1::SLOT_VALUES])[..., :experts]
        idx_tiles = (tok.astype(jnp.int32).transpose(2, 0, 1) + batch_row0).reshape(-1, 1, et)
        y = _experts(idx_tiles, x1p, slots, w_gate_bf, w_up_bf, w_down_bf, layer,
                     batch=batch, experts=experts, cap=cap, d=d, tm=et)
        xf, xb = _combine(p0_flat, y, pos, x1, ln2_g3, ln2_b3, layer,
                          alpha=alpha, batch=batch, seq=seq, cap=cap, experts=experts, tm=rt)
    return xf.reshape(batch, seq, d)
```

```python
import functools

import jax
import jax.numpy as jnp
from jax import lax
from jax.experimental import pallas as pl
from jax.experimental.pallas import tpu as pltpu

MLSTM_HEADS = 8
POOL_WINDOWS = (2, 4, 8, 16)
POOL_HALO = max(POOL_WINDOWS) // 2
EC_CAPACITY_FACTOR = 2
LN_EPS = 1e-5
GN_EPS = 1e-6

LANES = 128
SUBLANES = 8
VMEM_LIMIT_BYTES = 56 * 1024 * 1024

BF16 = jnp.bfloat16
F32 = jnp.float32


def _cparams(*semantics):
    return pltpu.CompilerParams(dimension_semantics=semantics, vmem_limit_bytes=VMEM_LIMIT_BYTES)


def _tile(n, want):
    t = min(n, want)
    while n % t:
        t //= 2
    return t


def _sigmoid(x):
    return 0.5 * jnp.tanh(0.5 * x) + 0.5


INPROJ_TILE_M = 2048
INPROJ_TILE_N = 1024


def _inproj_kernel(x_ref, w_ref, o_ref, *, sigmoid):
    acc = jnp.dot(x_ref[...], w_ref[0], preferred_element_type=F32)
    o_ref[...] = (_sigmoid(acc) if sigmoid else acc).astype(o_ref.dtype)


def _inproj(x_bf, w_bf, layer, *, col0, ncols, sigmoid):
    t, d = x_bf.shape
    n = ncols
    tm = _tile(t, INPROJ_TILE_M)
    tn = _tile(n, INPROJ_TILE_N)
    assert col0 % tn == 0
    return pl.pallas_call(
        functools.partial(_inproj_kernel, sigmoid=sigmoid),
        grid=(t // tm, n // tn),
        in_specs=[pl.BlockSpec((tm, d), lambda i, j: (i, 0)),
                  pl.BlockSpec((1, d, tn), lambda i, j: (layer, 0, col0 // tn + j))],
        out_specs=pl.BlockSpec((tm, tn), lambda i, j: (i, j)),
        out_shape=jax.ShapeDtypeStruct((t, n), BF16),
        compiler_params=_cparams("parallel", "arbitrary"),
        name="inproj",
    )(x_bf, w_bf)


def _split3(a):
    hi = a.astype(BF16)
    r1 = a - hi.astype(F32)
    mid = r1.astype(BF16)
    lo = (r1 - mid.astype(F32)).astype(BF16)
    return hi, mid, lo


def _gateprep_kernel(x_ref, w_ref, b_ref, col_ref, row_ref, *, heads):
    lk = x_ref.shape[0]
    g = jnp.dot(x_ref[...], w_ref[0], preferred_element_type=F32) + b_ref[0]
    logf = jax.nn.log_sigmoid(g)
    r = lax.broadcasted_iota(jnp.int32, (lk, lk), 0)
    c = lax.broadcasted_iota(jnp.int32, (lk, lk), 1)
    lower = (c <= r).astype(BF16)
    upper = (c >= r).astype(BF16)
    parts = _split3(logf)
    pre = sum(jnp.dot(lower, p, preferred_element_type=F32) for p in parts)
    suf = sum(jnp.dot(upper, p, preferred_element_type=F32) for p in parts)
    lane = lax.broadcasted_iota(jnp.int32, g.shape, 1)
    in_group = lambda k: (lane >= k * heads) & (lane < (k + 1) * heads)
    col = jnp.where(in_group(1), pre, jnp.where(in_group(3), suf, g))
    diff = col - pltpu.roll(col, LANES - heads, 1)
    row_id = lax.broadcasted_iota(jnp.int32, g.shape, 0)
    run_fwd = diff
    run_bwd = diff
    k = 1
    while k < lk:
        run_fwd = jnp.maximum(run_fwd, jnp.where(row_id >= k, pltpu.roll(run_fwd, k, 0), -jnp.inf))
        run_bwd = jnp.maximum(run_bwd, jnp.where(row_id < lk - k, pltpu.roll(run_bwd, lk - k, 0), -jnp.inf))
        k *= 2
    rowmax_fwd = pltpu.roll(col, 3 * heads, 1) + pltpu.roll(run_fwd, 4 * heads, 1)
    rowmax_bwd = pltpu.roll(col, 2 * heads, 1) + pltpu.roll(run_bwd, 3 * heads, 1)
    col = jnp.where(in_group(4), rowmax_fwd, jnp.where(in_group(5), rowmax_bwd, col))
    col_ref[...] = col
    row_ref[0] = col.T[: 4 * heads, :]


def _gateprep(x_bf, wg_bf, bias, layer, lk, heads):
    t, d = x_bf.shape
    nchunks = t // lk
    return pl.pallas_call(
        functools.partial(_gateprep_kernel, heads=heads),
        grid=(nchunks,),
        in_specs=[pl.BlockSpec((lk, d), lambda i: (i, 0)),
                  pl.BlockSpec((1, d, LANES), lambda i: (layer, 0, 0)),
                  pl.BlockSpec((1, 1, LANES), lambda i: (layer, 0, 0))],
        out_specs=[pl.BlockSpec((lk, LANES), lambda i: (i, 0)),
                   pl.BlockSpec((1, 4 * heads, lk), lambda i: (i, 0, 0))],
        out_shape=[jax.ShapeDtypeStruct((t, LANES), F32),
                   jax.ShapeDtypeStruct((nchunks, 4 * heads, lk), F32)],
        compiler_params=_cparams("parallel"),
        name="gateprep",
    )(x_bf, wg_bf, bias)


def _mlstm_chunk(q, k, v, i_col, b_col, rowmax_col, i_row, b_row, ct, n, m_prev, *, reverse):
    lk = q.shape[0]
    row = lax.broadcasted_iota(jnp.int32, (lk, lk), 0)
    col = lax.broadcasted_iota(jnp.int32, (lk, lk), 1)
    mask = (col >= row) if reverse else (col <= row)
    inter = b_col + m_prev
    m_t = jnp.maximum(inter, rowmax_col)
    w_inter = jnp.exp(inter - m_t)
    gates = jnp.where(mask, jnp.exp((b_col - m_t) + (i_row - b_row)), 0.0)
    a = gates * lax.dot_general(q, k, (((1,), (1,)), ((), ())), preferred_element_type=F32)
    num = (jnp.dot(a.astype(BF16), v, preferred_element_type=F32)
           + w_inter * jnp.dot(q, ct.astype(BF16), preferred_element_type=F32))
    qn = jnp.sum(q.astype(F32) * n.astype(BF16).astype(F32), axis=1, keepdims=True)
    den = jnp.sum(a, axis=1, keepdims=True) + w_inter * qn
    h = num * (1.0 / jnp.maximum(jnp.abs(den), jnp.exp(-m_t)))
    g = b_col[0:1, :] if reverse else b_col[lk - 1:lk, :]
    a_col = g - b_col + i_col
    m_new = jnp.maximum(g + m_prev, jnp.max(a_col, axis=0, keepdims=True))
    decay = jnp.exp(g + m_prev - m_new)
    wa = jnp.exp(a_col - m_new)
    vw = (v.astype(F32) * wa).astype(BF16)
    ct_new = decay * ct + lax.dot_general(k, vw, (((0,), (0,)), ((), ())), preferred_element_type=F32)
    n_new = decay * n + jnp.sum(wa.astype(BF16).astype(F32) * k.astype(F32), axis=0, keepdims=True)
    return h, ct_new, n_new, m_new


def _mlstm_kernel(*refs, heads, dh, reverse):
    if reverse:
        (q_ref, k_ref, v_ref, pc_ref, pr_ref, hf_ref, o_ref, nw_ref, out_ref, ct_ref, n_ref, m_ref) = refs
    else:
        (q_ref, k_ref, v_ref, pc_ref, pr_ref, out_ref, ct_ref, n_ref, m_ref) = refs

    @pl.when(pl.program_id(1) == 0)
    def _():
        ct_ref[...] = jnp.zeros_like(ct_ref)
        n_ref[...] = jnp.zeros_like(n_ref)
        m_ref[...] = jnp.zeros_like(m_ref)

    ig = 2 * heads if reverse else 0
    fg = ig + heads
    mg = (5 if reverse else 4) * heads
    pc = pc_ref[...]
    pr = pr_ref[0]
    for h in range(heads):
        sl = slice(h * dh, (h + 1) * dh)
        hh, ct_new, n_new, m_new = _mlstm_chunk(
            q_ref[:, sl], k_ref[:, sl], v_ref[:, sl],
            pc[:, ig + h:ig + h + 1], pc[:, fg + h:fg + h + 1], pc[:, mg + h:mg + h + 1],
            pr[ig + h:ig + h + 1, :], pr[fg + h:fg + h + 1, :],
            ct_ref[h], n_ref[h], m_ref[h], reverse=reverse)
        ct_ref[h] = ct_new
        n_ref[h] = n_new
        m_ref[h] = m_new
        if reverse:
            tot = hf_ref[:, sl] + hh
            mu = jnp.mean(tot, axis=1, keepdims=True)
            cen = tot - mu
            var = jnp.mean(cen * cen, axis=1, keepdims=True)
            hn = cen * lax.rsqrt(var + GN_EPS) * nw_ref[0, :, sl]
            out_ref[:, sl] = (_sigmoid(o_ref[:, sl].astype(F32)) * hn).astype(out_ref.dtype)
        else:
            out_ref[:, sl] = hh


def _mlstm(proj, pc, pr, col_q, col_k, col_v, col_o, hf, norm_w, layer, *, batch, seq, lk, heads, dh, reverse):
    t = batch * seq
    width = heads * dh
    nc = seq // lk
    chunk = (lambda b, c: b * nc + (nc - 1 - c)) if reverse else (lambda b, c: b * nc + c)
    tok_spec = lambda colblk: pl.BlockSpec((lk, width), lambda b, c: (chunk(b, c), colblk))
    in_specs = [tok_spec(col_q // width), tok_spec(col_k // width), tok_spec(col_v // width),
                pl.BlockSpec((lk, LANES), lambda b, c: (chunk(b, c), 0)),
                pl.BlockSpec((1, 4 * heads, lk), lambda b, c: (chunk(b, c), 0, 0))]
    args = [proj, proj, proj, pc, pr]
    if reverse:
        in_specs += [pl.BlockSpec((lk, width), lambda b, c: (chunk(b, c), 0)),
                     tok_spec(col_o // width),
                     pl.BlockSpec((1, 1, width), lambda b, c: (layer, 0, 0))]
        args += [hf, proj, norm_w]
        out_dtype = BF16
    else:
        out_dtype = F32
    return pl.pallas_call(
        functools.partial(_mlstm_kernel, heads=heads, dh=dh, reverse=reverse),
        grid=(batch, nc),
        in_specs=in_specs,
        out_specs=pl.BlockSpec((lk, width), lambda b, c: (chunk(b, c), 0)),
        out_shape=jax.ShapeDtypeStruct((t, width), out_dtype),
        scratch_shapes=[pltpu.VMEM((heads, dh, dh), F32),
                        pltpu.VMEM((heads, 1, dh), F32),
                        pltpu.VMEM((heads, 1, 1), F32)],
        compiler_params=_cparams("parallel", "arbitrary"),
        name="mlstm_bwd" if reverse else "mlstm_fwd",
    )(*args)


def _pool_kernel(prev_ref, cur_ref, next_ref, w_ref, s_ref, o_ref, *, seq, gd):
    tm = cur_ref.shape[0]
    i = pl.program_id(0)
    pos0 = (i * tm) % seq
    at_start = pos0 == 0
    at_end = pos0 + tm == seq
    prev = jnp.where(at_start, 0.0, prev_ref[...].astype(F32))
    nxt = jnp.where(at_end, 0.0, next_ref[...].astype(F32))
    cur = cur_ref[...].astype(F32)
    ext = jnp.concatenate([prev, cur, nxt], axis=0)
    n_ext = ext.shape[0]
    pos = pos0 + lax.broadcasted_iota(jnp.int32, (tm, 1), 0)
    for grp, win in enumerate(POOL_WINDOWS):
        e = ext[:, grp * gd:(grp + 1) * gd]
        acc = e + pltpu.roll(e, 1, 0)
        half = 1
        while 2 * half < win:
            acc = pltpu.roll(acc, n_ext - half, 0) + pltpu.roll(acc, half, 0)
            half *= 2
        wsum = acc[POOL_HALO:POOL_HALO + tm]
        lo = jnp.maximum(pos - win // 2, 0)
        hi = jnp.minimum(pos + win // 2, seq)
        cnt = (hi - lo).astype(F32)
        diff = wsum / cnt - cur[:, grp * gd:(grp + 1) * gd]
        y = jnp.dot(diff.astype(BF16), w_ref[0, grp], preferred_element_type=F32)
        o_ref[:, grp * gd:(grp + 1) * gd] = (y * s_ref[0, :, grp * gd:(grp + 1) * gd]).astype(o_ref.dtype)


def _pool(proj, w_grp_bf, scale, layer, *, seq, width, col_pool):
    t = proj.shape[0]
    _, ngrp, gd, _ = w_grp_bf.shape
    tm = _tile(seq, 512)
    hb = tm // POOL_HALO
    nhalo = t // POOL_HALO
    cb = col_pool // width
    return pl.pallas_call(
        functools.partial(_pool_kernel, seq=seq, gd=gd),
        grid=(t // tm,),
        in_specs=[pl.BlockSpec((POOL_HALO, width), lambda i: (jnp.maximum(i * hb - 1, 0), cb)),
                  pl.BlockSpec((tm, width), lambda i: (i, cb)),
                  pl.BlockSpec((POOL_HALO, width), lambda i: (jnp.minimum((i + 1) * hb, nhalo - 1), cb)),
                  pl.BlockSpec((1, ngrp, gd, gd), lambda i: (layer, 0, 0, 0)),
                  pl.BlockSpec((1, 1, width), lambda i: (layer, 0, 0))],
        out_specs=pl.BlockSpec((tm, width), lambda i: (i, 0)),
        out_shape=jax.ShapeDtypeStruct((t, width), BF16),
        compiler_params=_cparams("parallel"),
        name="pool",
    )(proj, proj, proj, w_grp_bf, scale)


def _merge_kernel(p_ref, m_ref, wp_ref, wm_ref, gp_ref, gm_ref, o_ref):
    yp = jnp.dot(p_ref[...], wp_ref[0], preferred_element_type=F32)
    ym = jnp.dot(m_ref[...], wm_ref[0], preferred_element_type=F32)
    o_ref[...] = (gp_ref[...].astype(F32) * yp + gm_ref[...].astype(F32) * ym).astype(o_ref.dtype)


def _merge(p, m, wp_bf, wm_bf, layer, proj, col_gp, col_gm):
    t, pw = p.shape
    mw = m.shape[1]
    d = wp_bf.shape[2]
    tm = _tile(t, 1024)
    tn = _tile(d, 1024)
    return pl.pallas_call(
        _merge_kernel,
        grid=(t // tm, d // tn),
        in_specs=[pl.BlockSpec((tm, pw), lambda i, j: (i, 0)),
                  pl.BlockSpec((tm, mw), lambda i, j: (i, 0)),
                  pl.BlockSpec((1, pw, tn), lambda i, j: (layer, 0, j)),
                  pl.BlockSpec((1, mw, tn), lambda i, j: (layer, 0, j)),
                  pl.BlockSpec((tm, tn), lambda i, j: (i, col_gp // tn + j)),
                  pl.BlockSpec((tm, tn), lambda i, j: (i, col_gm // tn + j))],
        out_specs=pl.BlockSpec((tm, tn), lambda i, j: (i, j)),
        out_shape=jax.ShapeDtypeStruct((t, d), BF16),
        compiler_params=_cparams("parallel", "arbitrary"),
        name="merge",
    )(p, m, wp_bf, wm_bf, proj, proj)


def _layer_norm_rows(y, g, b):
    mu = jnp.mean(y, axis=1, keepdims=True)
    cen = y - mu
    var = jnp.mean(cen * cen, axis=1, keepdims=True)
    return cen * lax.rsqrt(var + LN_EPS) * g + b


def _pack_rows(x_bf, xp_ref):
    tm, d = x_bf.shape
    half = d // 2
    nchunks = half // LANES
    bits = lax.bitcast_convert_type(x_bf.astype(F32), jnp.uint32)
    for j in range(nchunks):
        lo = jnp.right_shift(bits[:, j * LANES:(j + 1) * LANES], jnp.uint32(16))
        hi = jnp.bitwise_and(bits[:, half + j * LANES:half + (j + 1) * LANES], jnp.uint32(0xFFFF0000))
        xp_ref[pl.ds(j, tm, stride=nchunks), :] = lax.bitcast_convert_type(lo | hi, jnp.int32)


def _unpack_rows(buf_ref, tm, nchunks):
    lo, hi = [], []
    for j in range(nchunks):
        w = buf_ref[pl.ds(j, tm, stride=nchunks), :]
        lo.append(lax.bitcast_convert_type(jnp.left_shift(w, 16), F32).astype(BF16))
        hi.append(lax.bitcast_convert_type(jnp.bitwise_and(w, jnp.int32(-65536)), F32).astype(BF16))
    return jnp.concatenate(lo + hi, axis=1)


OUTPROJ_SUBTILE = 256


def _outproj_kernel(mix_ref, w_ref, x_ref, g_ref, b_ref, wr_ref, xo_ref, xp_ref, aff_ref, *, alpha, experts):
    tm = mix_ref.shape[0]
    sub = min(tm, OUTPROJ_SUBTILE)
    pack_rows = xp_ref.shape[0] // tm
    for r0 in range(0, tm, sub):
        rows = slice(r0, r0 + sub)
        mix = jnp.dot(mix_ref[rows, :], w_ref[0], preferred_element_type=F32)
        x1 = _layer_norm_rows(alpha * x_ref[rows, :] + mix, g_ref[0], b_ref[0])
        xo_ref[rows, :] = x1
        x1b = x1.astype(BF16)
        _pack_rows(x1b, xp_ref.at[pl.ds(r0 * pack_rows, sub * pack_rows), :])
        logits = jnp.dot(x1b, wr_ref[0], preferred_element_type=F32)
        lane = lax.broadcasted_iota(jnp.int32, logits.shape, 1)
        logits = jnp.where(lane < experts, logits, -jnp.inf)
        e = jnp.exp(logits - jnp.max(logits, axis=1, keepdims=True))
        aff_ref[rows, :] = e / jnp.sum(e, axis=1, keepdims=True)


def _outproj(mixin, wout_bf, x, ln_g, ln_b, wr_bf, layer, *, alpha, experts):
    t, d = x.shape
    tm = _tile(t, 512)
    pack_rows = d // (2 * LANES)
    assert pack_rows == SUBLANES, "one (8,128) int32 tile per packed token row"
    row = lambda i: (i, 0)
    lay = lambda i: (layer, 0, 0)
    return pl.pallas_call(
        functools.partial(_outproj_kernel, alpha=alpha, experts=experts),
        grid=(t // tm,),
        in_specs=[pl.BlockSpec((tm, d), row), pl.BlockSpec((1, d, d), lay), pl.BlockSpec((tm, d), row),
                  pl.BlockSpec((1, 1, d), lay), pl.BlockSpec((1, 1, d), lay), pl.BlockSpec((1, d, LANES), lay)],
        out_specs=[pl.BlockSpec((tm, d), row), pl.BlockSpec((tm * pack_rows, LANES), row),
                   pl.BlockSpec((tm, LANES), row)],
        out_shape=[jax.ShapeDtypeStruct((t, d), F32), jax.ShapeDtypeStruct((t * pack_rows, LANES), jnp.int32),
                   jax.ShapeDtypeStruct((t, LANES), F32)],
        compiler_params=_cparams("parallel"),
        name="outproj_ln",
    )(mixin, wout_bf, x, ln_g, ln_b, wr_bf)


ROUTE_TILE = 256
ROUTE_COUNT_CHUNK = 1024


def _route_kernel(aff_ref, pos_ref, p0_ref, *, cap, tm):
    s = aff_ref.shape[0]
    chunk = min(ROUTE_COUNT_CHUNK, s)

    def count_ge(cand):
        def body(c, acc):
            blk = aff_ref[pl.ds(pl.multiple_of(c * chunk, chunk), chunk), :]
            return acc + jnp.sum(jnp.where(blk >= cand, 1, 0), axis=0, keepdims=True)
        return lax.fori_loop(0, s // chunk, body, jnp.zeros((1, LANES), jnp.int32))

    def bit_body(k, thr_bits):
        cand = thr_bits | jnp.left_shift(jnp.int32(1), 30 - k)
        cnt = count_ge(lax.bitcast_convert_type(cand, F32))
        return jnp.where(cnt >= cap, cand, thr_bits)

    thr_bits = lax.fori_loop(0, 31, bit_body, jnp.zeros((1, LANES), jnp.int32))
    thr = lax.bitcast_convert_type(thr_bits, F32)
    n_gt = count_ge(lax.bitcast_convert_type(thr_bits + 1, F32))
    need = (cap - n_gt).astype(F32)

    r = lax.broadcasted_iota(jnp.int32, (tm, tm), 0)
    c = lax.broadcasted_iota(jnp.int32, (tm, tm), 1)
    tri = jnp.where(c < r, 1.0, 0.0).astype(BF16)

    def tile_body(j, carry):
        run_eq, run_sel = carry
        rows = pl.ds(pl.multiple_of(j * tm, tm), tm)
        blk = aff_ref[rows, :]
        gt = blk > thr
        eq = blk == thr
        eqf = jnp.where(eq, 1.0, 0.0)
        eq_rank = run_eq + jnp.dot(tri, eqf.astype(BF16), preferred_element_type=F32)
        sel = gt | (eq & (eq_rank < need))
        self_ = jnp.where(sel, 1.0, 0.0)
        pos = run_sel + jnp.dot(tri, self_.astype(BF16), preferred_element_type=F32)
        pos_ref[rows, :] = jnp.where(sel, pos, -1.0).astype(jnp.int32)
        p0_ref[0, pl.ds(j, 1), :] = run_sel.astype(jnp.int32)
        return (run_eq + jnp.sum(eqf, axis=0, keepdims=True), run_sel + jnp.sum(self_, axis=0, keepdims=True))

    zero = jnp.zeros((1, LANES), F32)
    lax.fori_loop(0, s // tm, tile_body, (zero, zero))


def _route(aff, *, batch, seq, cap, tm):
    t = aff.shape[0]
    ntiles = seq // tm
    return pl.pallas_call(
        functools.partial(_route_kernel, cap=cap, tm=tm),
        grid=(batch,),
        in_specs=[pl.BlockSpec((seq, LANES), lambda b: (b, 0))],
        out_specs=[pl.BlockSpec((seq, LANES), lambda b: (b, 0)),
                   pl.BlockSpec((1, ntiles, LANES), lambda b: (b, 0, 0))],
        out_shape=[jax.ShapeDtypeStruct((t, LANES), jnp.int32),
                   jax.ShapeDtypeStruct((batch, ntiles, LANES), jnp.int32)],
        compiler_params=_cparams("parallel"),
        name="route",
    )(aff)


SLOT_VALUES = 8
COMPACT_WINDOW = 128


def _compact_kernel(p0_ref, pos_ref, aff_ref, out_ref, *, experts, tm, ntiles, win):
    b = pl.program_id(0)
    j = pl.program_id(1)

    @pl.when(j == 0)
    def _():
        out_ref[...] = jnp.zeros_like(out_ref)

    post = pos_ref[...].astype(F32).T
    hi, mid, lo = _split3(aff_ref[...])
    rr = lax.broadcasted_iota(jnp.int32, (LANES, LANES), 0)
    cc = lax.broadcasted_iota(jnp.int32, (LANES, LANES), 1)
    spread = lambda k: jnp.where((cc == rr * SLOT_VALUES + k) & (rr < experts), 1.0, 0.0).astype(BF16)
    vals = (jnp.dot(hi, spread(2), preferred_element_type=F32)
            + jnp.dot(mid, spread(3), preferred_element_type=F32)
            + jnp.dot(lo, spread(4), preferred_element_type=F32))
    tok = j * tm + lax.broadcasted_iota(jnp.int32, (tm, LANES), 0)
    lane = lax.broadcasted_iota(jnp.int32, (tm, LANES), 1)
    k8 = lane % SLOT_VALUES
    vals = jnp.where(k8 == 0, (tok // LANES).astype(F32), jnp.where(k8 == 1, (tok % LANES).astype(F32), vals))
    vals = vals.astype(BF16)
    lane_expert = lane // SLOT_VALUES
    slot_iota = lax.broadcasted_iota(jnp.int32, (win, 1), 0)
    crowded = []
    for e in range(experts):
        base = (b * (ntiles + 1) + j) * experts + e
        p0 = p0_ref[base]
        p1 = p0_ref[base + experts]
        a0 = (p0 // SUBLANES) * SUBLANES
        nwin = (p1 - a0 + win - 1) // win
        ve = jnp.where(lane_expert == e, vals, jnp.zeros_like(vals))
        prow = post[e:e + 1, :]

        def add_window(w, carry, a0=a0, ve=ve, prow=prow):
            start = pl.multiple_of(a0 + w * win, SUBLANES)
            onehot = jnp.where(prow == (start + slot_iota).astype(F32), 1.0, 0.0).astype(BF16)
            out_ref[0, pl.ds(start, win), :] += jnp.dot(onehot, ve, preferred_element_type=F32)
            return carry

        add_window(0, 0)
        crowded.append((nwin, add_window))

    for nwin, add_window in crowded:
        lax.fori_loop(1, nwin, add_window, 0)


def _compact(p0_flat, pos, aff, *, batch, seq, cap, experts, tm):
    ntiles = seq // tm
    win = COMPACT_WINDOW
    cpad = cap + win
    tok = lambda b, j, p0: (b * ntiles + j, 0)
    return pl.pallas_call(
        functools.partial(_compact_kernel, experts=experts, tm=tm, ntiles=ntiles, win=win),
        grid_spec=pltpu.PrefetchScalarGridSpec(
            num_scalar_prefetch=1,
            grid=(batch, ntiles),
            in_specs=[pl.BlockSpec((tm, LANES), tok), pl.BlockSpec((tm, LANES), tok)],
            out_specs=pl.BlockSpec((1, cpad, LANES), lambda b, j, p0: (b, 0, 0))),
        out_shape=jax.ShapeDtypeStruct((batch, cpad, LANES), F32),
        compiler_params=_cparams("parallel", "arbitrary"),
        name="compact",
    )(p0_flat, pos, aff)


EXPERT_TILE = 512


def _expert_kernel(idx_ref, idx_next_ref, xp_hbm, slots_ref, wg_ref, wu_ref, wd_ref, y_ref, buf_ref, sem,
                   *, tm, pack_rows):
    e = pl.program_id(0)
    step = (e * pl.num_programs(1) + pl.program_id(1)) * pl.num_programs(2) + pl.program_id(2)
    last = pl.num_programs(0) * pl.num_programs(1) * pl.num_programs(2) - 1

    def start_rows(ids_ref, slot):
        for i in range(tm):
            src = xp_hbm.at[pl.ds(pl.multiple_of(ids_ref[0, 0, i] * pack_rows, pack_rows), pack_rows), :]
            pltpu.make_async_copy(src, buf_ref.at[slot, pl.ds(i * pack_rows, pack_rows), :], sem.at[slot]).start()

    def wait_rows(slot):
        pltpu.make_async_copy(xp_hbm.at[pl.ds(0, tm * pack_rows), :], buf_ref.at[slot], sem.at[slot]).wait()

    @pl.when(step == 0)
    def _():
        start_rows(idx_ref, 0)

    cur = step % 2
    start_rows(idx_next_ref, 1 - cur)
    wait_rows(cur)

    xe = _unpack_rows(buf_ref.at[cur], tm, pack_rows)
    g = jnp.dot(xe, wg_ref[0, 0], preferred_element_type=F32)
    u = jnp.dot(xe, wu_ref[0, 0], preferred_element_type=F32)
    hid = (g * _sigmoid(g) * u).astype(BF16)
    slots = slots_ref[0]
    lane = lax.broadcasted_iota(jnp.int32, slots.shape, 1)
    is_gate = (lane // SLOT_VALUES == e) & (lane % SLOT_VALUES >= 2) & (lane % SLOT_VALUES <= 4)
    gate = jnp.sum(jnp.where(is_gate, slots, 0.0), axis=1, keepdims=True)
    y = jnp.dot(hid, wd_ref[0, 0], preferred_element_type=F32) * gate
    y_ref[0, 0] = y.astype(y_ref.dtype)

    @pl.when(step == last)
    def _():
        wait_rows(1 - cur)


def _experts(idx_tiles, xp, slots, wg_bf, wu_bf, wd_bf, layer, *, batch, experts, cap, d, tm):
    f = wg_bf.shape[3]
    nct = cap // tm
    nsteps = experts * batch * nct
    pack_rows = d // (2 * LANES)
    step = lambda ei, bi, ci: (ei * batch + bi) * nct + ci
    wmap = lambda ei, bi, ci: (layer, ei, 0, 0)
    return pl.pallas_call(
        functools.partial(_expert_kernel, tm=tm, pack_rows=pack_rows),
        grid=(experts, batch, nct),
        in_specs=[pl.BlockSpec((1, 1, tm), lambda ei, bi, ci: (step(ei, bi, ci), 0, 0), memory_space=pltpu.SMEM),
                  pl.BlockSpec((1, 1, tm), lambda ei, bi, ci: (jnp.minimum(step(ei, bi, ci) + 1, nsteps - 1), 0, 0),
                               memory_space=pltpu.SMEM),
                  pl.BlockSpec(memory_space=pl.ANY),
                  pl.BlockSpec((1, tm, LANES), lambda ei, bi, ci: (bi, ci, 0)),
                  pl.BlockSpec((1, 1, d, f), wmap), pl.BlockSpec((1, 1, d, f), wmap),
                  pl.BlockSpec((1, 1, f, d), wmap)],
        out_specs=pl.BlockSpec((1, 1, tm, d), lambda ei, bi, ci: (bi, ei, ci, 0)),
        out_shape=jax.ShapeDtypeStruct((batch, experts, cap, d), BF16),
        scratch_shapes=[pltpu.VMEM((2, tm * pack_rows, LANES), jnp.int32), pltpu.SemaphoreType.DMA((2,))],
        compiler_params=_cparams("arbitrary", "arbitrary", "arbitrary"),
        name="experts",
    )(idx_tiles, idx_tiles, xp, slots, wg_bf, wu_bf, wd_bf)


COMBINE_SLAB = 64
POS_SPLIT_BITS = 6


def _combine_kernel(p0_ref, y_hbm, pos_ref, x_ref, g_ref, b_ref, spread_ref, xo_ref, xb_ref,
                    slab_ref, extra_ref, acc_ref, sem, sem_extra, *, alpha, experts, tm, ntiles, cap, slab):
    g = pl.program_id(0)
    nsteps = pl.num_programs(0)

    def first_slot(step, e):
        bb = step // ntiles
        p0 = p0_ref[(bb * (ntiles + 1) + step % ntiles) * experts + e]
        return bb, (p0 // SUBLANES) * SUBLANES

    def first_slab_copy(step, slot, e):
        bb, a0 = first_slot(step, e)
        return pltpu.make_async_copy(
            y_hbm.at[bb, e, pl.ds(pl.multiple_of(jnp.minimum(a0, cap - slab), SUBLANES), slab), :],
            slab_ref.at[slot, pl.ds(e * slab, slab), :], sem.at[slot])

    def fetch(step, slot):
        for e in range(experts):
            first_slab_copy(step, slot, e).start()

    @pl.when(g == 0)
    def _():
        fetch(0, 0)

    @pl.when(g + 1 < nsteps)
    def _():
        fetch(g + 1, (g + 1) % 2)

    cur = g % 2
    for e in range(experts):
        first_slab_copy(g, cur, e).wait()

    q = pos_ref[...] + 1
    q_hi = jnp.right_shift(q, POS_SPLIT_BITS).astype(F32).astype(BF16)
    q_lo = jnp.bitwise_and(q, (1 << POS_SPLIT_BITS) - 1).astype(F32).astype(BF16)
    qx = (jnp.dot(q_hi, spread_ref[...], preferred_element_type=F32) * float(1 << POS_SPLIT_BITS)
          + jnp.dot(q_lo, spread_ref[...], preferred_element_type=F32))
    col = lax.broadcasted_iota(jnp.int32, (1, experts * slab), 1)
    target = col % slab + 1
    for e in range(experts):
        _, a0 = first_slot(g, e)
        target = target + jnp.where(col // slab == e, jnp.minimum(a0, cap - slab), 0)
    onehot = jnp.where(qx == target.astype(F32), 1.0, 0.0).astype(BF16)
    acc_ref[...] = jnp.dot(onehot, slab_ref[cur], preferred_element_type=F32)

    slot_iota = lax.broadcasted_iota(jnp.int32, (1, slab), 1)
    for e in range(experts):
        bb, a0 = first_slot(g, e)
        p1 = p0_ref[(bb * (ntiles + 1) + g % ntiles + 1) * experts + e]
        nslab = (p1 - a0 + slab - 1) // slab
        pcol = pos_ref[:, e:e + 1]

        def more(w, carry, bb=bb, a0=a0, pcol=pcol, e=e):
            nominal = a0 + w * slab
            start = jnp.minimum(nominal, cap - slab)
            cp = pltpu.make_async_copy(y_hbm.at[bb, e, pl.ds(pl.multiple_of(start, SUBLANES), slab), :],
                                       extra_ref, sem_extra.at[0])
            cp.start()
            cp.wait()
            hit = (pcol == start + slot_iota) & (pcol >= nominal)
            acc_ref[...] += jnp.dot(jnp.where(hit, 1.0, 0.0).astype(BF16), extra_ref[...],
                                    preferred_element_type=F32)
            return carry

        lax.fori_loop(1, nslab, more, 0)

    x2 = _layer_norm_rows(alpha * x_ref[...] + acc_ref[...], g_ref[0], b_ref[0])
    xo_ref[...] = x2
    xb_ref[...] = x2.astype(BF16)


def _combine(p0_flat, y, pos, x, ln_g, ln_b, layer, *, alpha, batch, seq, cap, experts, tm):
    t, d = x.shape
    ntiles = seq // tm
    slab = min(COMBINE_SLAB, cap)
    assert cap >> POS_SPLIT_BITS < 256 and experts <= LANES
    lane_expert = jnp.arange(experts * slab, dtype=jnp.int32)[None, :] // slab
    spread = (lane_expert == jnp.arange(LANES, dtype=jnp.int32)[:, None]).astype(BF16)
    row = lambda i, p0: (i, 0)
    lay = lambda i, p0: (layer, 0, 0)
    return pl.pallas_call(
        functools.partial(_combine_kernel, alpha=alpha, experts=experts, tm=tm, ntiles=ntiles, cap=cap, slab=slab),
        grid_spec=pltpu.PrefetchScalarGridSpec(
            num_scalar_prefetch=1,
            grid=(t // tm,),
            in_specs=[pl.BlockSpec(memory_space=pl.ANY),
                      pl.BlockSpec((tm, LANES), row), pl.BlockSpec((tm, d), row),
                      pl.BlockSpec((1, 1, d), lay), pl.BlockSpec((1, 1, d), lay),
                      pl.BlockSpec((LANES, experts * slab), lambda i, p0: (0, 0))],
            out_specs=[pl.BlockSpec((tm, d), row), pl.BlockSpec((tm, d), row)],
            scratch_shapes=[pltpu.VMEM((2, experts * slab, d), BF16),
                            pltpu.VMEM((slab, d), BF16),
                            pltpu.VMEM((tm, d), F32),
                            pltpu.SemaphoreType.DMA((2,)),
                            pltpu.SemaphoreType.DMA((1,))]),
        out_shape=[jax.ShapeDtypeStruct((t, d), F32), jax.ShapeDtypeStruct((t, d), BF16)],
        compiler_params=_cparams("arbitrary"),
        name="combine_ln2",
    )(p0_flat, y, pos, x, ln_g, ln_b, spread)


def kernel(x, w_in, b_if, w_pool_grp, pool_scale, w_pool_up, mlstm_norm_w, w_mlstm_up, w_out, ln1_g, ln1_b,
           w_router, w_gate, w_up, w_down, ln2_g, ln2_b):
    batch, seq, d = x.shape
    depth = w_in.shape[0]
    t = batch * seq
    heads = MLSTM_HEADS
    pw = w_pool_up.shape[1]
    mw = w_mlstm_up.shape[1]
    dh = mw // heads
    experts = w_router.shape[2]
    cap = EC_CAPACITY_FACTOR * seq // experts
    ngates = 4 * heads
    alpha = (2.0 * depth) ** 0.25
    lk = _tile(seq, 256)
    rt = _tile(seq, ROUTE_TILE)
    et = _tile(cap, EXPERT_TILE)
    ref_q = pw
    ref_k = ref_q + mw
    ref_if = pw + 4 * mw
    ref_gp = ref_if + ngates
    col_q, col_k, col_v, col_o = 0, mw, 2 * mw, 3 * mw
    col_gp, col_gm = 0, d

    cols = jnp.arange(w_in.shape[2])
    q_scale = jnp.where((cols >= ref_q) & (cols < ref_k), dh ** -0.5, 1.0).astype(F32)
    w_in_bf = lax.optimization_barrier((w_in * q_scale).astype(BF16))
    w_branch_gates = w_in_bf[:, :, ref_gp:]
    w_gates = jnp.concatenate([w_in[:, :, ref_if:ref_gp], jnp.zeros((depth, d, LANES - ngates), F32)],
                              axis=2).astype(BF16)
    gate_bias = jnp.concatenate([b_if, jnp.zeros((depth, LANES - ngates), F32)], axis=1).reshape(depth, 1, LANES)
    w_router_bf = jnp.concatenate([w_router, jnp.zeros((depth, d, LANES - experts), F32)], axis=2).astype(BF16)
    w_pool_grp_bf = w_pool_grp.astype(BF16)
    w_pool_up_bf = w_pool_up.astype(BF16)
    w_mlstm_up_bf = w_mlstm_up.astype(BF16)
    w_out_bf = w_out.astype(BF16)
    w_gate_bf = w_gate.astype(BF16)
    w_up_bf = w_up.astype(BF16)
    w_down_bf = w_down.astype(BF16)
    row3 = lambda a: a.reshape(depth, 1, a.shape[-1])
    pool_scale3, norm_w3 = row3(pool_scale), row3(mlstm_norm_w)
    ln1_g3, ln1_b3, ln2_g3, ln2_b3 = row3(ln1_g), row3(ln1_b), row3(ln2_g), row3(ln2_b)
    batch_row0 = (jnp.arange(batch, dtype=jnp.int32) * seq)[None, :, None]

    xf = x.reshape(t, d)
    xb = xf.astype(BF16)
    for layer in range(depth):
        proj = _inproj(xb, w_in_bf, layer, col0=ref_q, ncols=4 * mw, sigmoid=False)
        branch_gates = _inproj(xb, w_branch_gates, layer, col0=0, ncols=2 * d, sigmoid=True)
        pool_in = _inproj(xb, w_in_bf, layer, col0=0, ncols=pw, sigmoid=False)
        pc, pr = _gateprep(xb, w_gates, gate_bias, layer, lk, heads)
        common = dict(batch=batch, seq=seq, lk=lk, heads=heads, dh=dh)
        hf = _mlstm(proj, pc, pr, col_q, col_k, col_v, col_o, None, None, layer, reverse=False, **common)
        m = _mlstm(proj, pc, pr, col_q, col_k, col_v, col_o, hf, norm_w3, layer, reverse=True, **common)
        p = _pool(pool_in, w_pool_grp_bf, pool_scale3, layer, seq=seq, width=pw, col_pool=0)
        mixin = _merge(p, m, w_pool_up_bf, w_mlstm_up_bf, layer, branch_gates, col_gp, col_gm)
        x1, x1p, aff = _outproj(mixin, w_out_bf, xf, ln1_g3, ln1_b3, w_router_bf, layer, alpha=alpha, experts=experts)

        pos, p0 = _route(aff, batch=batch, seq=seq, cap=cap, tm=rt)
        p0_flat = jnp.concatenate([p0[:, :, :experts], jnp.full((batch, 1, experts), cap, jnp.int32)],
                                  axis=1).reshape(-1)
        slots = _compact(p0_flat, pos, aff, batch=batch, seq=seq, cap=cap, experts=experts, tm=rt)
        tok = (slots[:, :cap, 0::SLOT_VALUES] * LANES + slots[:, :cap, 1::SLOT_VALUES])[..., :experts]
        idx_tiles = (tok.astype(jnp.int32).transpose(2, 0, 1) + batch_row0).reshape(-1, 1, et)
        y = _experts(idx_tiles, x1p, slots, w_gate_bf, w_up_bf, w_down_bf, layer,
                     batch=batch, experts=experts, cap=cap, d=d, tm=et)
        xf, xb = _combine(p0_flat, y, pos, x1, ln2_g3, ln2_b3, layer,
                          alpha=alpha, batch=batch, seq=seq, cap=cap, experts=experts, tm=rt)
    return xf.reshape(batch, seq, d)
```

```python
import functools

import jax
import jax.numpy as jnp
from jax import lax
from jax.experimental import pallas as pl
from jax.experimental.pallas import tpu as pltpu

MLSTM_HEADS = 8
POOL_WINDOWS = (2, 4, 8, 16)
POOL_HALO = max(POOL_WINDOWS) // 2
EC_CAPACITY_FACTOR = 2
LN_EPS = 1e-5
GN_EPS = 1e-6

LANES = 128
SUBLANES = 8
VMEM_LIMIT_BYTES = 56 * 1024 * 1024

BF16 = jnp.bfloat16
F32 = jnp.float32


def _cparams(*semantics):
    return pltpu.CompilerParams(dimension_semantics=semantics, vmem_limit_bytes=VMEM_LIMIT_BYTES)


def _tile(n, want):
    t = min(n, want)
    while n % t:
        t //= 2
    return t


WCAST_ROWS = 128


def _wcast_kernel(w_ref, s_ref, o_ref, tail_ref, *, tail0):
    w = (w_ref[0] * s_ref[...]).astype(BF16)
    o_ref[0] = w
    tail_ref[0] = w[:, tail0:]


def _wcast(w, col_scale, tail0):
    depth, d, n = w.shape
    rows = _tile(d, WCAST_ROWS)
    blk = lambda l, i: (l, i, 0)
    return pl.pallas_call(
        functools.partial(_wcast_kernel, tail0=tail0),
        grid=(depth, d // rows),
        in_specs=[pl.BlockSpec((1, rows, n), blk), pl.BlockSpec((1, n), lambda l, i: (0, 0))],
        out_specs=[pl.BlockSpec((1, rows, n), blk), pl.BlockSpec((1, rows, n - tail0), blk)],
        out_shape=[jax.ShapeDtypeStruct((depth, d, n), BF16), jax.ShapeDtypeStruct((depth, d, n - tail0), BF16)],
        compiler_params=_cparams("parallel", "parallel"),
        name="wcast",
    )(w, col_scale)


def _sigmoid(x):
    return 0.5 * jnp.tanh(0.5 * x) + 0.5


INPROJ_TILE_M = 2048
INPROJ_TILE_N = 1024


def _inproj_kernel(x_ref, w_ref, o_ref, *, sigmoid):
    acc = jnp.dot(x_ref[...], w_ref[0], preferred_element_type=F32)
    o_ref[...] = (_sigmoid(acc) if sigmoid else acc).astype(o_ref.dtype)


def _inproj(x_bf, w_bf, layer, *, col0, ncols, sigmoid):
    t, d = x_bf.shape
    n = ncols
    tm = _tile(t, INPROJ_TILE_M)
    tn = _tile(n, INPROJ_TILE_N)
    assert col0 % tn == 0
    return pl.pallas_call(
        functools.partial(_inproj_kernel, sigmoid=sigmoid),
        grid=(t // tm, n // tn),
        in_specs=[pl.BlockSpec((tm, d), lambda i, j: (i, 0)),
                  pl.BlockSpec((1, d, tn), lambda i, j: (layer, 0, col0 // tn + j))],
        out_specs=pl.BlockSpec((tm, tn), lambda i, j: (i, j)),
        out_shape=jax.ShapeDtypeStruct((t, n), BF16),
        compiler_params=_cparams("parallel", "arbitrary"),
        name="inproj",
    )(x_bf, w_bf)


def _split3(a):
    hi = a.astype(BF16)
    r1 = a - hi.astype(F32)
    mid = r1.astype(BF16)
    lo = (r1 - mid.astype(F32)).astype(BF16)
    return hi, mid, lo


def _gateprep_kernel(x_ref, w_ref, b_ref, col_ref, row_ref, *, heads):
    lk = x_ref.shape[0]
    g = jnp.dot(x_ref[...], w_ref[0], preferred_element_type=F32) + b_ref[0]
    logf = jax.nn.log_sigmoid(g)
    r = lax.broadcasted_iota(jnp.int32, (lk, lk), 0)
    c = lax.broadcasted_iota(jnp.int32, (lk, lk), 1)
    lower = (c <= r).astype(BF16)
    upper = (c >= r).astype(BF16)
    parts = _split3(logf)
    pre = sum(jnp.dot(lower, p, preferred_element_type=F32) for p in parts)
    suf = sum(jnp.dot(upper, p, preferred_element_type=F32) for p in parts)
    lane = lax.broadcasted_iota(jnp.int32, g.shape, 1)
    in_group = lambda k: (lane >= k * heads) & (lane < (k + 1) * heads)
    col = jnp.where(in_group(1), pre, jnp.where(in_group(3), suf, g))
    diff = col - pltpu.roll(col, LANES - heads, 1)
    row_id = lax.broadcasted_iota(jnp.int32, g.shape, 0)
    run_fwd = diff
    run_bwd = diff
    k = 1
    while k < lk:
        run_fwd = jnp.maximum(run_fwd, jnp.where(row_id >= k, pltpu.roll(run_fwd, k, 0), -jnp.inf))
        run_bwd = jnp.maximum(run_bwd, jnp.where(row_id < lk - k, pltpu.roll(run_bwd, lk - k, 0), -jnp.inf))
        k *= 2
    rowmax_fwd = pltpu.roll(col, 3 * heads, 1) + pltpu.roll(run_fwd, 4 * heads, 1)
    rowmax_bwd = pltpu.roll(col, 2 * heads, 1) + pltpu.roll(run_bwd, 3 * heads, 1)
    col = jnp.where(in_group(4), rowmax_fwd, jnp.where(in_group(5), rowmax_bwd, col))
    col_ref[...] = col
    row_ref[0] = col.T[: 4 * heads, :]


def _gateprep(x_bf, wg_bf, bias, layer, lk, heads):
    t, d = x_bf.shape
    nchunks = t // lk
    return pl.pallas_call(
        functools.partial(_gateprep_kernel, heads=heads),
        grid=(nchunks,),
        in_specs=[pl.BlockSpec((lk, d), lambda i: (i, 0)),
                  pl.BlockSpec((1, d, LANES), lambda i: (layer, 0, 0)),
                  pl.BlockSpec((1, 1, LANES), lambda i: (layer, 0, 0))],
        out_specs=[pl.BlockSpec((lk, LANES), lambda i: (i, 0)),
                   pl.BlockSpec((1, 4 * heads, lk), lambda i: (i, 0, 0))],
        out_shape=[jax.ShapeDtypeStruct((t, LANES), F32),
                   jax.ShapeDtypeStruct((nchunks, 4 * heads, lk), F32)],
        compiler_params=_cparams("parallel"),
        name="gateprep",
    )(x_bf, wg_bf, bias)


def _mlstm_chunk(q, k, v, i_col, b_col, rowmax_col, i_row, b_row, ct, n, m_prev, *, reverse):
    lk = q.shape[0]
    row = lax.broadcasted_iota(jnp.int32, (lk, lk), 0)
    col = lax.broadcasted_iota(jnp.int32, (lk, lk), 1)
    mask = (col >= row) if reverse else (col <= row)
    inter = b_col + m_prev
    m_t = jnp.maximum(inter, rowmax_col)
    w_inter = jnp.exp(inter - m_t)
    gates = jnp.where(mask, jnp.exp((b_col - m_t) + (i_row - b_row)), 0.0)
    a = gates * lax.dot_general(q, k, (((1,), (1,)), ((), ())), preferred_element_type=F32)
    num = (jnp.dot(a.astype(BF16), v, preferred_element_type=F32)
           + w_inter * jnp.dot(q, ct.astype(BF16), preferred_element_type=F32))
    qn = jnp.sum(q.astype(F32) * n.astype(BF16).astype(F32), axis=1, keepdims=True)
    den = jnp.sum(a, axis=1, keepdims=True) + w_inter * qn
    h = num * (1.0 / jnp.maximum(jnp.abs(den), jnp.exp(-m_t)))
    g = b_col[0:1, :] if reverse else b_col[lk - 1:lk, :]
    a_col = g - b_col + i_col
    m_new = jnp.maximum(g + m_prev, jnp.max(a_col, axis=0, keepdims=True))
    decay = jnp.exp(g + m_prev - m_new)
    wa = jnp.exp(a_col - m_new)
    vw = (v.astype(F32) * wa).astype(BF16)
    ct_new = decay * ct + lax.dot_general(k, vw, (((0,), (0,)), ((), ())), preferred_element_type=F32)
    n_new = decay * n + jnp.sum(wa.astype(BF16).astype(F32) * k.astype(F32), axis=0, keepdims=True)
    return h, ct_new, n_new, m_new


def _mlstm_kernel(*refs, heads, dh, reverse):
    if reverse:
        (q_ref, k_ref, v_ref, pc_ref, pr_ref, hf_ref, o_ref, nw_ref, out_ref, ct_ref, n_ref, m_ref) = refs
    else:
        (q_ref, k_ref, v_ref, pc_ref, pr_ref, out_ref, ct_ref, n_ref, m_ref) = refs

    @pl.when(pl.program_id(1) == 0)
    def _():
        ct_ref[...] = jnp.zeros_like(ct_ref)
        n_ref[...] = jnp.zeros_like(n_ref)
        m_ref[...] = jnp.zeros_like(m_ref)

    ig = 2 * heads if reverse else 0
    fg = ig + heads
    mg = (5 if reverse else 4) * heads
    pc = pc_ref[...]
    pr = pr_ref[0]
    for h in range(heads):
        sl = slice(h * dh, (h + 1) * dh)
        hh, ct_new, n_new, m_new = _mlstm_chunk(
            q_ref[:, sl], k_ref[:, sl], v_ref[:, sl],
            pc[:, ig + h:ig + h + 1], pc[:, fg + h:fg + h + 1], pc[:, mg + h:mg + h + 1],
            pr[ig + h:ig + h + 1, :], pr[fg + h:fg + h + 1, :],
            ct_ref[h], n_ref[h], m_ref[h], reverse=reverse)
        ct_ref[h] = ct_new
        n_ref[h] = n_new
        m_ref[h] = m_new
        if reverse:
            tot = hf_ref[:, sl] + hh
            mu = jnp.mean(tot, axis=1, keepdims=True)
            cen = tot - mu
            var = jnp.mean(cen * cen, axis=1, keepdims=True)
            hn = cen * lax.rsqrt(var + GN_EPS) * nw_ref[0, :, sl]
            out_ref[:, sl] = (_sigmoid(o_ref[:, sl].astype(F32)) * hn).astype(out_ref.dtype)
        else:
            out_ref[:, sl] = hh


def _mlstm(proj, pc, pr, col_q, col_k, col_v, col_o, hf, norm_w, layer, *, batch, seq, lk, heads, dh, reverse):
    t = batch * seq
    width = heads * dh
    nc = seq // lk
    chunk = (lambda b, c: b * nc + (nc - 1 - c)) if reverse else (lambda b, c: b * nc + c)
    tok_spec = lambda colblk: pl.BlockSpec((lk, width), lambda b, c: (chunk(b, c), colblk))
    in_specs = [tok_spec(col_q // width), tok_spec(col_k // width), tok_spec(col_v // width),
                pl.BlockSpec((lk, LANES), lambda b, c: (chunk(b, c), 0)),
                pl.BlockSpec((1, 4 * heads, lk), lambda b, c: (chunk(b, c), 0, 0))]
    args = [proj, proj, proj, pc, pr]
    if reverse:
        in_specs += [pl.BlockSpec((lk, width), lambda b, c: (chunk(b, c), 0)),
                     tok_spec(col_o // width),
                     pl.BlockSpec((1, 1, width), lambda b, c: (layer, 0, 0))]
        args += [hf, proj, norm_w]
        out_dtype = BF16
    else:
        out_dtype = F32
    return pl.pallas_call(
        functools.partial(_mlstm_kernel, heads=heads, dh=dh, reverse=reverse),
        grid=(batch, nc),
        in_specs=in_specs,
        out_specs=pl.BlockSpec((lk, width), lambda b, c: (chunk(b, c), 0)),
        out_shape=jax.ShapeDtypeStruct((t, width), out_dtype),
        scratch_shapes=[pltpu.VMEM((heads, dh, dh), F32),
                        pltpu.VMEM((heads, 1, dh), F32),
                        pltpu.VMEM((heads, 1, 1), F32)],
        compiler_params=_cparams("parallel", "arbitrary"),
        name="mlstm_bwd" if reverse else "mlstm_fwd",
    )(*args)


def _pool_kernel(prev_ref, cur_ref, next_ref, w_ref, s_ref, o_ref, *, seq, gd):
    tm = cur_ref.shape[0]
    i = pl.program_id(0)
    pos0 = (i * tm) % seq
    at_start = pos0 == 0
    at_end = pos0 + tm == seq
    prev = jnp.where(at_start, 0.0, prev_ref[...].astype(F32))
    nxt = jnp.where(at_end, 0.0, next_ref[...].astype(F32))
    cur = cur_ref[...].astype(F32)
    ext = jnp.concatenate([prev, cur, nxt], axis=0)
    n_ext = ext.shape[0]
    pos = pos0 + lax.broadcasted_iota(jnp.int32, (tm, 1), 0)
    for grp, win in enumerate(POOL_WINDOWS):
        e = ext[:, grp * gd:(grp + 1) * gd]
        acc = e + pltpu.roll(e, 1, 0)
        half = 1
        while 2 * half < win:
            acc = pltpu.roll(acc, n_ext - half, 0) + pltpu.roll(acc, half, 0)
            half *= 2
        wsum = acc[POOL_HALO:POOL_HALO + tm]
        lo = jnp.maximum(pos - win // 2, 0)
        hi = jnp.minimum(pos + win // 2, seq)
        cnt = (hi - lo).astype(F32)
        diff = wsum / cnt - cur[:, grp * gd:(grp + 1) * gd]
        y = jnp.dot(diff.astype(BF16), w_ref[0, grp], preferred_element_type=F32)
        o_ref[:, grp * gd:(grp + 1) * gd] = (y * s_ref[0, :, grp * gd:(grp + 1) * gd]).astype(o_ref.dtype)


def _pool(proj, w_grp_bf, scale, layer, *, seq, width, col_pool):
    t = proj.shape[0]
    _, ngrp, gd, _ = w_grp_bf.shape
    tm = _tile(seq, 512)
    hb = tm // POOL_HALO
    nhalo = t // POOL_HALO
    cb = col_pool // width
    return pl.pallas_call(
        functools.partial(_pool_kernel, seq=seq, gd=gd),
        grid=(t // tm,),
        in_specs=[pl.BlockSpec((POOL_HALO, width), lambda i: (jnp.maximum(i * hb - 1, 0), cb)),
                  pl.BlockSpec((tm, width), lambda i: (i, cb)),
                  pl.BlockSpec((POOL_HALO, width), lambda i: (jnp.minimum((i + 1) * hb, nhalo - 1), cb)),
                  pl.BlockSpec((1, ngrp, gd, gd), lambda i: (layer, 0, 0, 0)),
                  pl.BlockSpec((1, 1, width), lambda i: (layer, 0, 0))],
        out_specs=pl.BlockSpec((tm, width), lambda i: (i, 0)),
        out_shape=jax.ShapeDtypeStruct((t, width), BF16),
        compiler_params=_cparams("parallel"),
        name="pool",
    )(proj, proj, proj, w_grp_bf, scale)


def _merge_kernel(p_ref, m_ref, wp_ref, wm_ref, gp_ref, gm_ref, o_ref):
    yp = jnp.dot(p_ref[...], wp_ref[0], preferred_element_type=F32)
    ym = jnp.dot(m_ref[...], wm_ref[0], preferred_element_type=F32)
    o_ref[...] = (gp_ref[...].astype(F32) * yp + gm_ref[...].astype(F32) * ym).astype(o_ref.dtype)


def _merge(p, m, wp_bf, wm_bf, layer, proj, col_gp, col_gm):
    t, pw = p.shape
    mw = m.shape[1]
    d = wp_bf.shape[2]
    tm = _tile(t, 1024)
    tn = _tile(d, 1024)
    return pl.pallas_call(
        _merge_kernel,
        grid=(t // tm, d // tn),
        in_specs=[pl.BlockSpec((tm, pw), lambda i, j: (i, 0)),
                  pl.BlockSpec((tm, mw), lambda i, j: (i, 0)),
                  pl.BlockSpec((1, pw, tn), lambda i, j: (layer, 0, j)),
                  pl.BlockSpec((1, mw, tn), lambda i, j: (layer, 0, j)),
                  pl.BlockSpec((tm, tn), lambda i, j: (i, col_gp // tn + j)),
                  pl.BlockSpec((tm, tn), lambda i, j: (i, col_gm // tn + j))],
        out_specs=pl.BlockSpec((tm, tn), lambda i, j: (i, j)),
        out_shape=jax.ShapeDtypeStruct((t, d), BF16),
        compiler_params=_cparams("parallel", "arbitrary"),
        name="merge",
    )(p, m, wp_bf, wm_bf, proj, proj)


def _layer_norm_rows(y, g, b):
    mu = jnp.mean(y, axis=1, keepdims=True)
    cen = y - mu
    var = jnp.mean(cen * cen, axis=1, keepdims=True)
    return cen * lax.rsqrt(var + LN_EPS) * g + b


def _pack_rows(x_bf, xp_ref):
    tm, d = x_bf.shape
    half = d // 2
    nchunks = half // LANES
    bits = lax.bitcast_convert_type(x_bf.astype(F32), jnp.uint32)
    for j in range(nchunks):
        lo = jnp.right_shift(bits[:, j * LANES:(j + 1) * LANES], jnp.uint32(16))
        hi = jnp.bitwise_and(bits[:, half + j * LANES:half + (j + 1) * LANES], jnp.uint32(0xFFFF0000))
        xp_ref[pl.ds(j, tm, stride=nchunks), :] = lax.bitcast_convert_type(lo | hi, jnp.int32)


def _unpack_rows(buf_ref, tm, nchunks):
    lo, hi = [], []
    for j in range(nchunks):
        w = buf_ref[pl.ds(j, tm, stride=nchunks), :]
        lo.append(lax.bitcast_convert_type(jnp.left_shift(w, 16), F32).astype(BF16))
        hi.append(lax.bitcast_convert_type(jnp.bitwise_and(w, jnp.int32(-65536)), F32).astype(BF16))
    return jnp.concatenate(lo + hi, axis=1)


OUTPROJ_SUBTILE = 256


def _outproj_kernel(mix_ref, w_ref, x_ref, g_ref, b_ref, wr_ref, xo_ref, xp_ref, aff_ref, *, alpha, experts):
    tm = mix_ref.shape[0]
    sub = min(tm, OUTPROJ_SUBTILE)
    pack_rows = xp_ref.shape[0] // tm
    for r0 in range(0, tm, sub):
        rows = slice(r0, r0 + sub)
        mix = jnp.dot(mix_ref[rows, :], w_ref[0], preferred_element_type=F32)
        x1 = _layer_norm_rows(alpha * x_ref[rows, :] + mix, g_ref[0], b_ref[0])
        xo_ref[rows, :] = x1
        x1b = x1.astype(BF16)
        _pack_rows(x1b, xp_ref.at[pl.ds(r0 * pack_rows, sub * pack_rows), :])
        logits = jnp.dot(x1b, wr_ref[0], preferred_element_type=F32)
        lane = lax.broadcasted_iota(jnp.int32, logits.shape, 1)
        logits = jnp.where(lane < experts, logits, -jnp.inf)
        e = jnp.exp(logits - jnp.max(logits, axis=1, keepdims=True))
        aff_ref[rows, :] = e / jnp.sum(e, axis=1, keepdims=True)


def _outproj(mixin, wout_bf, x, ln_g, ln_b, wr_bf, layer, *, alpha, experts):
    t, d = x.shape
    tm = _tile(t, 512)
    pack_rows = d // (2 * LANES)
    assert pack_rows == SUBLANES, "one (8,128) int32 tile per packed token row"
    row = lambda i: (i, 0)
    lay = lambda i: (layer, 0, 0)
    return pl.pallas_call(
        functools.partial(_outproj_kernel, alpha=alpha, experts=experts),
        grid=(t // tm,),
        in_specs=[pl.BlockSpec((tm, d), row), pl.BlockSpec((1, d, d), lay), pl.BlockSpec((tm, d), row),
                  pl.BlockSpec((1, 1, d), lay), pl.BlockSpec((1, 1, d), lay), pl.BlockSpec((1, d, LANES), lay)],
        out_specs=[pl.BlockSpec((tm, d), row), pl.BlockSpec((tm * pack_rows, LANES), row),
                   pl.BlockSpec((tm, LANES), row)],
        out_shape=[jax.ShapeDtypeStruct((t, d), F32), jax.ShapeDtypeStruct((t * pack_rows, LANES), jnp.int32),
                   jax.ShapeDtypeStruct((t, LANES), F32)],
        compiler_params=_cparams("parallel"),
        name="outproj_ln",
    )(mixin, wout_bf, x, ln_g, ln_b, wr_bf)


ROUTE_TILE = 512
ROUTE_COUNT_CHUNK = 1024


def _route_kernel(aff_ref, pos_ref, p0_ref, *, cap, tm):
    s = aff_ref.shape[0]
    chunk = min(ROUTE_COUNT_CHUNK, s)

    def count_ge(cand):
        def body(c, acc):
            blk = aff_ref[pl.ds(pl.multiple_of(c * chunk, chunk), chunk), :]
            return acc + jnp.sum(jnp.where(blk >= cand, 1, 0), axis=0, keepdims=True)
        return lax.fori_loop(0, s // chunk, body, jnp.zeros((1, LANES), jnp.int32))

    def bit_body(k, thr_bits):
        cand = thr_bits | jnp.left_shift(jnp.int32(1), 30 - k)
        cnt = count_ge(lax.bitcast_convert_type(cand, F32))
        return jnp.where(cnt >= cap, cand, thr_bits)

    thr_bits = lax.fori_loop(0, 31, bit_body, jnp.zeros((1, LANES), jnp.int32))
    thr = lax.bitcast_convert_type(thr_bits, F32)
    n_gt = count_ge(lax.bitcast_convert_type(thr_bits + 1, F32))
    need = (cap - n_gt).astype(F32)

    r = lax.broadcasted_iota(jnp.int32, (tm, tm), 0)
    c = lax.broadcasted_iota(jnp.int32, (tm, tm), 1)
    tri = jnp.where(c < r, 1.0, 0.0).astype(BF16)

    def tile_body(j, carry):
        run_eq, run_sel = carry
        rows = pl.ds(pl.multiple_of(j * tm, tm), tm)
        blk = aff_ref[rows, :]
        gt = blk > thr
        eq = blk == thr
        eqf = jnp.where(eq, 1.0, 0.0)
        eq_rank = run_eq + jnp.dot(tri, eqf.astype(BF16), preferred_element_type=F32)
        sel = gt | (eq & (eq_rank < need))
        self_ = jnp.where(sel, 1.0, 0.0)
        pos = run_sel + jnp.dot(tri, self_.astype(BF16), preferred_element_type=F32)
        pos_ref[rows, :] = jnp.where(sel, pos, -1.0).astype(jnp.int32)
        p0_ref[0, pl.ds(j, 1), :] = run_sel.astype(jnp.int32)
        return (run_eq + jnp.sum(eqf, axis=0, keepdims=True), run_sel + jnp.sum(self_, axis=0, keepdims=True))

    zero = jnp.zeros((1, LANES), F32)
    lax.fori_loop(0, s // tm, tile_body, (zero, zero))


def _route(aff, *, batch, seq, cap, tm):
    t = aff.shape[0]
    ntiles = seq // tm
    return pl.pallas_call(
        functools.partial(_route_kernel, cap=cap, tm=tm),
        grid=(batch,),
        in_specs=[pl.BlockSpec((seq, LANES), lambda b: (b, 0))],
        out_specs=[pl.BlockSpec((seq, LANES), lambda b: (b, 0)),
                   pl.BlockSpec((1, ntiles, LANES), lambda b: (b, 0, 0))],
        out_shape=[jax.ShapeDtypeStruct((t, LANES), jnp.int32),
                   jax.ShapeDtypeStruct((batch, ntiles, LANES), jnp.int32)],
        compiler_params=_cparams("parallel"),
        name="route",
    )(aff)


SLOT_VALUES = 8
COMPACT_WINDOW = 128


def _compact_kernel(p0_ref, pos_ref, aff_ref, out_ref, *, experts, tm, ntiles, win):
    b = pl.program_id(0)
    j = pl.program_id(1)

    @pl.when(j == 0)
    def _():
        out_ref[...] = jnp.zeros_like(out_ref)

    post = pos_ref[...].astype(F32).T
    hi, mid, lo = _split3(aff_ref[...])
    rr = lax.broadcasted_iota(jnp.int32, (LANES, LANES), 0)
    cc = lax.broadcasted_iota(jnp.int32, (LANES, LANES), 1)
    spread = lambda k: jnp.where((cc == rr * SLOT_VALUES + k) & (rr < experts), 1.0, 0.0).astype(BF16)
    vals = (jnp.dot(hi, spread(2), preferred_element_type=F32)
            + jnp.dot(mid, spread(3), preferred_element_type=F32)
            + jnp.dot(lo, spread(4), preferred_element_type=F32))
    tok = j * tm + lax.broadcasted_iota(jnp.int32, (tm, LANES), 0)
    lane = lax.broadcasted_iota(jnp.int32, (tm, LANES), 1)
    k8 = lane % SLOT_VALUES
    vals = jnp.where(k8 == 0, (tok // LANES).astype(F32), jnp.where(k8 == 1, (tok % LANES).astype(F32), vals))
    vals = vals.astype(BF16)
    lane_expert = lane // SLOT_VALUES
    slot_iota = lax.broadcasted_iota(jnp.int32, (win, 1), 0)
    crowded = []
    for e in range(experts):
        base = (b * (ntiles + 1) + j) * experts + e
        p0 = p0_ref[base]
        p1 = p0_ref[base + experts]
        a0 = (p0 // SUBLANES) * SUBLANES
        nwin = (p1 - a0 + win - 1) // win
        ve = jnp.where(lane_expert == e, vals, jnp.zeros_like(vals))
        prow = post[e:e + 1, :]

        def add_window(w, carry, a0=a0, ve=ve, prow=prow):
            start = pl.multiple_of(a0 + w * win, SUBLANES)
            onehot = jnp.where(prow == (start + slot_iota).astype(F32), 1.0, 0.0).astype(BF16)
            out_ref[0, pl.ds(start, win), :] += jnp.dot(onehot, ve, preferred_element_type=F32)
            return carry

        add_window(0, 0)
        crowded.append((nwin, add_window))

    for nwin, add_window in crowded:
        lax.fori_loop(1, nwin, add_window, 0)


def _compact(p0_flat, pos, aff, *, batch, seq, cap, experts, tm):
    ntiles = seq // tm
    win = COMPACT_WINDOW
    cpad = cap + win
    tok = lambda b, j, p0: (b * ntiles + j, 0)
    return pl.pallas_call(
        functools.partial(_compact_kernel, experts=experts, tm=tm, ntiles=ntiles, win=win),
        grid_spec=pltpu.PrefetchScalarGridSpec(
            num_scalar_prefetch=1,
            grid=(batch, ntiles),
            in_specs=[pl.BlockSpec((tm, LANES), tok), pl.BlockSpec((tm, LANES), tok)],
            out_specs=pl.BlockSpec((1, cpad, LANES), lambda b, j, p0: (b, 0, 0))),
        out_shape=jax.ShapeDtypeStruct((batch, cpad, LANES), F32),
        compiler_params=_cparams("parallel", "arbitrary"),
        name="compact",
    )(p0_flat, pos, aff)


EXPERT_TILE = 512


def _expert_kernel(idx_ref, idx_next_ref, xp_hbm, slots_ref, wg_ref, wu_ref, wd_ref, y_ref, buf_ref, sem,
                   *, tm, pack_rows):
    e = pl.program_id(0)
    step = (e * pl.num_programs(1) + pl.program_id(1)) * pl.num_programs(2) + pl.program_id(2)
    last = pl.num_programs(0) * pl.num_programs(1) * pl.num_programs(2) - 1

    def start_rows(ids_ref, slot):
        for i in range(tm):
            src = xp_hbm.at[pl.ds(pl.multiple_of(ids_ref[0, 0, i] * pack_rows, pack_rows), pack_rows), :]
            pltpu.make_async_copy(src, buf_ref.at[slot, pl.ds(i * pack_rows, pack_rows), :], sem.at[slot]).start()

    def wait_rows(slot):
        pltpu.make_async_copy(xp_hbm.at[pl.ds(0, tm * pack_rows), :], buf_ref.at[slot], sem.at[slot]).wait()

    @pl.when(step == 0)
    def _():
        start_rows(idx_ref, 0)

    cur = step % 2
    start_rows(idx_next_ref, 1 - cur)
    wait_rows(cur)

    xe = _unpack_rows(buf_ref.at[cur], tm, pack_rows)
    g = jnp.dot(xe, wg_ref[0, 0], preferred_element_type=F32)
    u = jnp.dot(xe, wu_ref[0, 0], preferred_element_type=F32)
    hid = (g * _sigmoid(g) * u).astype(BF16)
    slots = slots_ref[0]
    lane = lax.broadcasted_iota(jnp.int32, slots.shape, 1)
    is_gate = (lane // SLOT_VALUES == e) & (lane % SLOT_VALUES >= 2) & (lane % SLOT_VALUES <= 4)
    gate = jnp.sum(jnp.where(is_gate, slots, 0.0), axis=1, keepdims=True)
    y = jnp.dot(hid, wd_ref[0, 0], preferred_element_type=F32) * gate
    y_ref[0, 0] = y.astype(y_ref.dtype)

    @pl.when(step == last)
    def _():
        wait_rows(1 - cur)


def _experts(idx_tiles, xp, slots, wg_bf, wu_bf, wd_bf, layer, *, batch, experts, cap, d, tm):
    f = wg_bf.shape[3]
    nct = cap // tm
    nsteps = experts * batch * nct
    pack_rows = d // (2 * LANES)
    step = lambda ei, bi, ci: (ei * batch + bi) * nct + ci
    wmap = lambda ei, bi, ci: (layer, ei, 0, 0)
    return pl.pallas_call(
        functools.partial(_expert_kernel, tm=tm, pack_rows=pack_rows),
        grid=(experts, batch, nct),
        in_specs=[pl.BlockSpec((1, 1, tm), lambda ei, bi, ci: (step(ei, bi, ci), 0, 0), memory_space=pltpu.SMEM),
                  pl.BlockSpec((1, 1, tm), lambda ei, bi, ci: (jnp.minimum(step(ei, bi, ci) + 1, nsteps - 1), 0, 0),
                               memory_space=pltpu.SMEM),
                  pl.BlockSpec(memory_space=pl.ANY),
                  pl.BlockSpec((1, tm, LANES), lambda ei, bi, ci: (bi, ci, 0)),
                  pl.BlockSpec((1, 1, d, f), wmap), pl.BlockSpec((1, 1, d, f), wmap),
                  pl.BlockSpec((1, 1, f, d), wmap)],
        out_specs=pl.BlockSpec((1, 1, tm, d), lambda ei, bi, ci: (bi, ei, ci, 0)),
        out_shape=jax.ShapeDtypeStruct((batch, experts, cap, d), BF16),
        scratch_shapes=[pltpu.VMEM((2, tm * pack_rows, LANES), jnp.int32), pltpu.SemaphoreType.DMA((2,))],
        compiler_params=_cparams("arbitrary", "arbitrary", "arbitrary"),
        name="experts",
    )(idx_tiles, idx_tiles, xp, slots, wg_bf, wu_bf, wd_bf)


COMBINE_SLAB = 96
POS_SPLIT_BITS = 6


def _combine_kernel(p0_ref, y_hbm, pos_ref, x_ref, g_ref, b_ref, spread_ref, xo_ref, xb_ref,
                    slab_ref, extra_ref, acc_ref, sem, sem_extra, *, alpha, experts, tm, ntiles, cap, slab):
    g = pl.program_id(0)
    nsteps = pl.num_programs(0)

    def first_slot(step, e):
        bb = step // ntiles
        p0 = p0_ref[(bb * (ntiles + 1) + step % ntiles) * experts + e]
        return bb, (p0 // SUBLANES) * SUBLANES

    def first_slab_copy(step, slot, e):
        bb, a0 = first_slot(step, e)
        return pltpu.make_async_copy(
            y_hbm.at[bb, e, pl.ds(pl.multiple_of(jnp.minimum(a0, cap - slab), SUBLANES), slab), :],
            slab_ref.at[slot, pl.ds(e * slab, slab), :], sem.at[slot])

    def fetch(step, slot):
        for e in range(experts):
            first_slab_copy(step, slot, e).start()

    @pl.when(g == 0)
    def _():
        fetch(0, 0)

    @pl.when(g + 1 < nsteps)
    def _():
        fetch(g + 1, (g + 1) % 2)

    cur = g % 2
    for e in range(experts):
        first_slab_copy(g, cur, e).wait()

    q = pos_ref[...] + 1
    q_hi = jnp.right_shift(q, POS_SPLIT_BITS).astype(F32).astype(BF16)
    q_lo = jnp.bitwise_and(q, (1 << POS_SPLIT_BITS) - 1).astype(F32).astype(BF16)
    qx = (jnp.dot(q_hi, spread_ref[...], preferred_element_type=F32) * float(1 << POS_SPLIT_BITS)
          + jnp.dot(q_lo, spread_ref[...], preferred_element_type=F32))
    col = lax.broadcasted_iota(jnp.int32, (1, experts * slab), 1)
    target = col % slab + 1
    for e in range(experts):
        _, a0 = first_slot(g, e)
        target = target + jnp.where(col // slab == e, jnp.minimum(a0, cap - slab), 0)
    onehot = jnp.where(qx == target.astype(F32), 1.0, 0.0).astype(BF16)
    acc_ref[...] = jnp.dot(onehot, slab_ref[cur], preferred_element_type=F32)

    slot_iota = lax.broadcasted_iota(jnp.int32, (1, slab), 1)
    for e in range(experts):
        bb, a0 = first_slot(g, e)
        p1 = p0_ref[(bb * (ntiles + 1) + g % ntiles + 1) * experts + e]
        nslab = (p1 - a0 + slab - 1) // slab
        pcol = pos_ref[:, e:e + 1]

        def more(w, carry, bb=bb, a0=a0, pcol=pcol, e=e):
            nominal = a0 + w * slab
            start = jnp.minimum(nominal, cap - slab)
            cp = pltpu.make_async_copy(y_hbm.at[bb, e, pl.ds(pl.multiple_of(start, SUBLANES), slab), :],
                                       extra_ref, sem_extra.at[0])
            cp.start()
            cp.wait()
            hit = (pcol == start + slot_iota) & (pcol >= nominal)
            acc_ref[...] += jnp.dot(jnp.where(hit, 1.0, 0.0).astype(BF16), extra_ref[...],
                                    preferred_element_type=F32)
            return carry

        lax.fori_loop(1, nslab, more, 0)

    x2 = _layer_norm_rows(alpha * x_ref[...] + acc_ref[...], g_ref[0], b_ref[0])
    xo_ref[...] = x2
    xb_ref[...] = x2.astype(BF16)


def _combine(p0_flat, y, pos, x, ln_g, ln_b, layer, *, alpha, batch, seq, cap, experts, tm):
    t, d = x.shape
    ntiles = seq // tm
    slab = min(COMBINE_SLAB, cap)
    assert cap >> POS_SPLIT_BITS < 256 and experts <= LANES
    lane_expert = jnp.arange(experts * slab, dtype=jnp.int32)[None, :] // slab
    spread = (lane_expert == jnp.arange(LANES, dtype=jnp.int32)[:, None]).astype(BF16)
    row = lambda i, p0: (i, 0)
    lay = lambda i, p0: (layer, 0, 0)
    return pl.pallas_call(
        functools.partial(_combine_kernel, alpha=alpha, experts=experts, tm=tm, ntiles=ntiles, cap=cap, slab=slab),
        grid_spec=pltpu.PrefetchScalarGridSpec(
            num_scalar_prefetch=1,
            grid=(t // tm,),
            in_specs=[pl.BlockSpec(memory_space=pl.ANY),
                      pl.BlockSpec((tm, LANES), row), pl.BlockSpec((tm, d), row),
                      pl.BlockSpec((1, 1, d), lay), pl.BlockSpec((1, 1, d), lay),
                      pl.BlockSpec((LANES, experts * slab), lambda i, p0: (0, 0))],
            out_specs=[pl.BlockSpec((tm, d), row), pl.BlockSpec((tm, d), row)],
            scratch_shapes=[pltpu.VMEM((2, experts * slab, d), BF16),
                            pltpu.VMEM((slab, d), BF16),
                            pltpu.VMEM((tm, d), F32),
                            pltpu.SemaphoreType.DMA((2,)),
                            pltpu.SemaphoreType.DMA((1,))]),
        out_shape=[jax.ShapeDtypeStruct((t, d), F32), jax.ShapeDtypeStruct((t, d), BF16)],
        compiler_params=_cparams("arbitrary"),
        name="combine_ln2",
    )(p0_flat, y, pos, x, ln_g, ln_b, spread)


def kernel(x, w_in, b_if, w_pool_grp, pool_scale, w_pool_up, mlstm_norm_w, w_mlstm_up, w_out, ln1_g, ln1_b,
           w_router, w_gate, w_up, w_down, ln2_g, ln2_b):
    batch, seq, d = x.shape
    depth = w_in.shape[0]
    t = batch * seq
    heads = MLSTM_HEADS
    pw = w_pool_up.shape[1]
    mw = w_mlstm_up.shape[1]
    dh = mw // heads
    experts = w_router.shape[2]
    cap = EC_CAPACITY_FACTOR * seq // experts
    ngates = 4 * heads
    alpha = (2.0 * depth) ** 0.25
    lk = _tile(seq, 256)
    rt = _tile(seq, ROUTE_TILE)
    et = _tile(cap, EXPERT_TILE)
    ref_q = pw
    ref_k = ref_q + mw
    ref_if = pw + 4 * mw
    ref_gp = ref_if + ngates
    col_q, col_k, col_v, col_o = 0, mw, 2 * mw, 3 * mw
    col_gp, col_gm = 0, d

    cols = jnp.arange(w_in.shape[2])
    q_scale = jnp.where((cols >= ref_q) & (cols < ref_k), dh ** -0.5, 1.0).astype(F32).reshape(1, -1)
    w_in_bf, w_branch_gates = _wcast(w_in, q_scale, ref_gp)
    w_gates = jnp.concatenate([w_in[:, :, ref_if:ref_gp], jnp.zeros((depth, d, LANES - ngates), F32)],
                              axis=2).astype(BF16)
    gate_bias = jnp.concatenate([b_if, jnp.zeros((depth, LANES - ngates), F32)], axis=1).reshape(depth, 1, LANES)
    w_router_bf = jnp.concatenate([w_router, jnp.zeros((depth, d, LANES - experts), F32)], axis=2).astype(BF16)
    w_pool_grp_bf = w_pool_grp.astype(BF16)
    w_pool_up_bf = w_pool_up.astype(BF16)
    w_mlstm_up_bf = w_mlstm_up.astype(BF16)
    w_out_bf = w_out.astype(BF16)
    w_gate_bf = w_gate.astype(BF16)
    w_up_bf = w_up.astype(BF16)
    w_down_bf = w_down.astype(BF16)
    row3 = lambda a: a.reshape(depth, 1, a.shape[-1])
    pool_scale3, norm_w3 = row3(pool_scale), row3(mlstm_norm_w)
    ln1_g3, ln1_b3, ln2_g3, ln2_b3 = row3(ln1_g), row3(ln1_b), row3(ln2_g), row3(ln2_b)
    batch_row0 = (jnp.arange(batch, dtype=jnp.int32) * seq)[None, :, None]

    xf = x.reshape(t, d)
    xb = xf.astype(BF16)
    for layer in range(depth):
        proj = _inproj(xb, w_in_bf, layer, col0=ref_q, ncols=4 * mw, sigmoid=False)
        branch_gates = _inproj(xb, w_branch_gates, layer, col0=0, ncols=2 * d, sigmoid=True)
        pool_in = _inproj(xb, w_in_bf, layer, col0=0, ncols=pw, sigmoid=False)
        pc, pr = _gateprep(xb, w_gates, gate_bias, layer, lk, heads)
        common = dict(batch=batch, seq=seq, lk=lk, heads=heads, dh=dh)
        hf = _mlstm(proj, pc, pr, col_q, col_k, col_v, col_o, None, None, layer, reverse=False, **common)
        m = _mlstm(proj, pc, pr, col_q, col_k, col_v, col_o, hf, norm_w3, layer, reverse=True, **common)
        p = _pool(pool_in, w_pool_grp_bf, pool_scale3, layer, seq=seq, width=pw, col_pool=0)
        mixin = _merge(p, m, w_pool_up_bf, w_mlstm_up_bf, layer, branch_gates, col_gp, col_gm)
        x1, x1p, aff = _outproj(mixin, w_out_bf, xf, ln1_g3, ln1_b3, w_router_bf, layer, alpha=alpha, experts=experts)

        pos, p0 = _route(aff, batch=batch, seq=seq, cap=cap, tm=rt)
        p0_flat = jnp.concatenate([p0[:, :, :experts], jnp.full((batch, 1, experts), cap, jnp.int32)],
                                  axis=1).reshape(-1)
        slots = _compact(p0_flat, pos, aff, batch=batch, seq=seq, cap=cap, experts=experts, tm=rt)
        tok = (slots[:, :cap, 0::SLOT_VALUES] * LANES + slots[:, :cap, 1::SLOT_VALUES])[..., :experts]
        idx_tiles = (tok.astype(jnp.int32).transpose(2, 0, 1) + batch_row0).reshape(-1, 1, et)
        y = _experts(idx_tiles, x1p, slots, w_gate_bf, w_up_bf, w_down_bf, layer,
                     batch=batch, experts=experts, cap=cap, d=d, tm=et)
        xf, xb = _combine(p0_flat, y, pos, x1, ln2_g3, ln2_b3, layer,
                          alpha=alpha, batch=batch, seq=seq, cap=cap, experts=experts, tm=rt)
    return xf.reshape(batch, seq, d)
```

```python
import functools

import jax
import jax.numpy as jnp
from jax import lax
from jax.experimental import pallas as pl
from jax.experimental.pallas import tpu as pltpu

MLSTM_HEADS = 8
POOL_WINDOWS = (2, 4, 8, 16)
POOL_HALO = max(POOL_WINDOWS) // 2
EC_CAPACITY_FACTOR = 2
LN_EPS = 1e-5
GN_EPS = 1e-6

LANES = 128
SUBLANES = 8
VMEM_LIMIT_BYTES = 56 * 1024 * 1024

BF16 = jnp.bfloat16
F32 = jnp.float32


def _cparams(*semantics):
    return pltpu.CompilerParams(dimension_semantics=semantics, vmem_limit_bytes=VMEM_LIMIT_BYTES)


def _tile(n, want):
    t = min(n, want)
    while n % t:
        t //= 2
    return t


WCAST_COLS = 512


def _wcast_kernel(wt_ref, o_ref, *, scale_lo, scale_hi, scale):
    cb = wt_ref.shape[1]
    col = pl.program_id(1) * cb + lax.broadcasted_iota(jnp.int32, (cb, 1), 0)
    w = wt_ref[0] * jnp.where((col >= scale_lo) & (col < scale_hi), scale, 1.0)
    o_ref[0] = w.T.astype(BF16)


def _wcast(w, scale_lo, scale_hi, scale):
    depth, d, n = w.shape
    cb = WCAST_COLS
    return pl.pallas_call(
        functools.partial(_wcast_kernel, scale_lo=scale_lo, scale_hi=scale_hi, scale=scale),
        grid=(depth, pl.cdiv(n, cb)),
        in_specs=[pl.BlockSpec((1, cb, d), lambda l, j: (l, j, 0))],
        out_specs=pl.BlockSpec((1, d, cb), lambda l, j: (l, 0, j)),
        out_shape=jax.ShapeDtypeStruct((depth, d, n), BF16),
        compiler_params=_cparams("parallel", "parallel"),
        name="wcast",
    )(jnp.swapaxes(w, 1, 2))


def _sigmoid(x):
    return 0.5 * jnp.tanh(0.5 * x) + 0.5


INPROJ_TILE_M = 2048
INPROJ_TILE_N = 1024


def _inproj_kernel(x_ref, w_ref, o_ref, *, sigmoid):
    acc = jnp.dot(x_ref[...], w_ref[0], preferred_element_type=F32)
    o_ref[...] = (_sigmoid(acc) if sigmoid else acc).astype(o_ref.dtype)


def _inproj(x_bf, w_bf, layer, *, col0, ncols, sigmoid):
    t, d = x_bf.shape
    n = ncols
    tm = _tile(t, INPROJ_TILE_M)
    tn = _tile(n, INPROJ_TILE_N)
    assert col0 % tn == 0
    return pl.pallas_call(
        functools.partial(_inproj_kernel, sigmoid=sigmoid),
        grid=(t // tm, n // tn),
        in_specs=[pl.BlockSpec((tm, d), lambda i, j: (i, 0)),
                  pl.BlockSpec((1, d, tn), lambda i, j: (layer, 0, col0 // tn + j))],
        out_specs=pl.BlockSpec((tm, tn), lambda i, j: (i, j)),
        out_shape=jax.ShapeDtypeStruct((t, n), BF16),
        compiler_params=_cparams("parallel", "arbitrary"),
        name="inproj",
    )(x_bf, w_bf)


def _split3(a):
    hi = a.astype(BF16)
    r1 = a - hi.astype(F32)
    mid = r1.astype(BF16)
    lo = (r1 - mid.astype(F32)).astype(BF16)
    return hi, mid, lo


def _gateprep_kernel(x_ref, w_ref, b_ref, col_ref, row_ref, *, heads):
    lk = x_ref.shape[0]
    g = jnp.dot(x_ref[...], w_ref[0], preferred_element_type=F32) + b_ref[0]
    logf = jax.nn.log_sigmoid(g)
    r = lax.broadcasted_iota(jnp.int32, (lk, lk), 0)
    c = lax.broadcasted_iota(jnp.int32, (lk, lk), 1)
    lower = (c <= r).astype(BF16)
    upper = (c >= r).astype(BF16)
    parts = _split3(logf)
    pre = sum(jnp.dot(lower, p, preferred_element_type=F32) for p in parts)
    suf = sum(jnp.dot(upper, p, preferred_element_type=F32) for p in parts)
    lane = lax.broadcasted_iota(jnp.int32, g.shape, 1)
    in_group = lambda k: (lane >= k * heads) & (lane < (k + 1) * heads)
    col = jnp.where(in_group(1), pre, jnp.where(in_group(3), suf, g))
    diff = col - pltpu.roll(col, LANES - heads, 1)
    row_id = lax.broadcasted_iota(jnp.int32, g.shape, 0)
    run_fwd = diff
    run_bwd = diff
    k = 1
    while k < lk:
        run_fwd = jnp.maximum(run_fwd, jnp.where(row_id >= k, pltpu.roll(run_fwd, k, 0), -jnp.inf))
        run_bwd = jnp.maximum(run_bwd, jnp.where(row_id < lk - k, pltpu.roll(run_bwd, lk - k, 0), -jnp.inf))
        k *= 2
    rowmax_fwd = pltpu.roll(col, 3 * heads, 1) + pltpu.roll(run_fwd, 4 * heads, 1)
    rowmax_bwd = pltpu.roll(col, 2 * heads, 1) + pltpu.roll(run_bwd, 3 * heads, 1)
    col = jnp.where(in_group(4), rowmax_fwd, jnp.where(in_group(5), rowmax_bwd, col))
    col_ref[...] = col
    row_ref[0] = col.T[: 4 * heads, :]


def _gateprep(x_bf, wg_bf, bias, layer, lk, heads):
    t, d = x_bf.shape
    nchunks = t // lk
    return pl.pallas_call(
        functools.partial(_gateprep_kernel, heads=heads),
        grid=(nchunks,),
        in_specs=[pl.BlockSpec((lk, d), lambda i: (i, 0)),
                  pl.BlockSpec((1, d, LANES), lambda i: (layer, 0, 0)),
                  pl.BlockSpec((1, 1, LANES), lambda i: (layer, 0, 0))],
        out_specs=[pl.BlockSpec((lk, LANES), lambda i: (i, 0)),
                   pl.BlockSpec((1, 4 * heads, lk), lambda i: (i, 0, 0))],
        out_shape=[jax.ShapeDtypeStruct((t, LANES), F32),
                   jax.ShapeDtypeStruct((nchunks, 4 * heads, lk), F32)],
        compiler_params=_cparams("parallel"),
        name="gateprep",
    )(x_bf, wg_bf, bias)


MLSTM_CHUNKS_PER_STEP = 2


def _mlstm_chunk(q, k, v, i_col, b_col, rowmax_col, i_row, b_row, ct, n, m_prev, *, reverse):
    lk = q.shape[0]
    row = lax.broadcasted_iota(jnp.int32, (lk, lk), 0)
    col = lax.broadcasted_iota(jnp.int32, (lk, lk), 1)
    mask = (col >= row) if reverse else (col <= row)
    inter = b_col + m_prev
    m_t = jnp.maximum(inter, rowmax_col)
    w_inter = jnp.exp(inter - m_t)
    gates = jnp.where(mask, jnp.exp((b_col - m_t) + (i_row - b_row)), 0.0)
    a = gates * lax.dot_general(q, k, (((1,), (1,)), ((), ())), preferred_element_type=F32)
    num = (jnp.dot(a.astype(BF16), v, preferred_element_type=F32)
           + w_inter * jnp.dot(q, ct.astype(BF16), preferred_element_type=F32))
    qn = jnp.sum(q.astype(F32) * n.astype(BF16).astype(F32), axis=1, keepdims=True)
    den = jnp.sum(a, axis=1, keepdims=True) + w_inter * qn
    h = num * (1.0 / jnp.maximum(jnp.abs(den), jnp.exp(-m_t)))
    g = b_col[0:1, :] if reverse else b_col[lk - 1:lk, :]
    a_col = g - b_col + i_col
    m_new = jnp.maximum(g + m_prev, jnp.max(a_col, axis=0, keepdims=True))
    decay = jnp.exp(g + m_prev - m_new)
    wa = jnp.exp(a_col - m_new)
    vw = (v.astype(F32) * wa).astype(BF16)
    ct_new = decay * ct + lax.dot_general(k, vw, (((0,), (0,)), ((), ())), preferred_element_type=F32)
    n_new = decay * n + jnp.sum(wa.astype(BF16).astype(F32) * k.astype(F32), axis=0, keepdims=True)
    return h, ct_new, n_new, m_new


def _mlstm_kernel(*refs, heads, dh, reverse):
    if reverse:
        (q_ref, k_ref, v_ref, pc_ref, pr_ref, hf_ref, o_ref, nw_ref, out_ref, ct_ref, n_ref, m_ref) = refs
    else:
        (q_ref, k_ref, v_ref, pc_ref, pr_ref, out_ref, ct_ref, n_ref, m_ref) = refs

    @pl.when(pl.program_id(1) == 0)
    def _():
        ct_ref[...] = jnp.zeros_like(ct_ref)
        n_ref[...] = jnp.zeros_like(n_ref)
        m_ref[...] = jnp.zeros_like(m_ref)

    ig = 2 * heads if reverse else 0
    fg = ig + heads
    mg = (5 if reverse else 4) * heads
    cps = pr_ref.shape[0]
    lk = pr_ref.shape[2]
    for cc in (range(cps - 1, -1, -1) if reverse else range(cps)):
        rows = slice(cc * lk, (cc + 1) * lk)
        pc = pc_ref[rows, :]
        pr = pr_ref[cc]
        for h in range(heads):
            sl = slice(h * dh, (h + 1) * dh)
            hh, ct_new, n_new, m_new = _mlstm_chunk(
                q_ref[rows, sl], k_ref[rows, sl], v_ref[rows, sl],
                pc[:, ig + h:ig + h + 1], pc[:, fg + h:fg + h + 1], pc[:, mg + h:mg + h + 1],
                pr[ig + h:ig + h + 1, :], pr[fg + h:fg + h + 1, :],
                ct_ref[h], n_ref[h], m_ref[h], reverse=reverse)
            ct_ref[h] = ct_new
            n_ref[h] = n_new
            m_ref[h] = m_new
            if reverse:
                tot = hf_ref[rows, sl] + hh
                mu = jnp.mean(tot, axis=1, keepdims=True)
                cen = tot - mu
                var = jnp.mean(cen * cen, axis=1, keepdims=True)
                hn = cen * lax.rsqrt(var + GN_EPS) * nw_ref[0, :, sl]
                out_ref[rows, sl] = (_sigmoid(o_ref[rows, sl].astype(F32)) * hn).astype(out_ref.dtype)
            else:
                out_ref[rows, sl] = hh


def _mlstm(proj, pc, pr, col_q, col_k, col_v, col_o, hf, norm_w, layer, *, batch, seq, lk, heads, dh, reverse):
    t = batch * seq
    width = heads * dh
    cps = MLSTM_CHUNKS_PER_STEP if (seq // lk) % MLSTM_CHUNKS_PER_STEP == 0 else 1
    nc = seq // (lk * cps)
    blk = lk * cps
    chunk = (lambda b, c: b * nc + (nc - 1 - c)) if reverse else (lambda b, c: b * nc + c)
    tok_spec = lambda colblk: pl.BlockSpec((blk, width), lambda b, c: (chunk(b, c), colblk))
    in_specs = [tok_spec(col_q // width), tok_spec(col_k // width), tok_spec(col_v // width),
                pl.BlockSpec((blk, LANES), lambda b, c: (chunk(b, c), 0)),
                pl.BlockSpec((cps, 4 * heads, lk), lambda b, c: (chunk(b, c), 0, 0))]
    args = [proj, proj, proj, pc, pr]
    if reverse:
        in_specs += [pl.BlockSpec((blk, width), lambda b, c: (chunk(b, c), 0)),
                     tok_spec(col_o // width),
                     pl.BlockSpec((1, 1, width), lambda b, c: (layer, 0, 0))]
        args += [hf, proj, norm_w]
        out_dtype = BF16
    else:
        out_dtype = F32
    return pl.pallas_call(
        functools.partial(_mlstm_kernel, heads=heads, dh=dh, reverse=reverse),
        grid=(batch, nc),
        in_specs=in_specs,
        out_specs=pl.BlockSpec((blk, width), lambda b, c: (chunk(b, c), 0)),
        out_shape=jax.ShapeDtypeStruct((t, width), out_dtype),
        scratch_shapes=[pltpu.VMEM((heads, dh, dh), F32),
                        pltpu.VMEM((heads, 1, dh), F32),
                        pltpu.VMEM((heads, 1, 1), F32)],
        compiler_params=_cparams("parallel", "arbitrary"),
        name="mlstm_bwd" if reverse else "mlstm_fwd",
    )(*args)


def _pool_kernel(prev_ref, cur_ref, next_ref, w_ref, s_ref, o_ref, *, seq, gd):
    tm = cur_ref.shape[0]
    i = pl.program_id(0)
    pos0 = (i * tm) % seq
    at_start = pos0 == 0
    at_end = pos0 + tm == seq
    prev = jnp.where(at_start, 0.0, prev_ref[...].astype(F32))
    nxt = jnp.where(at_end, 0.0, next_ref[...].astype(F32))
    cur = cur_ref[...].astype(F32)
    ext = jnp.concatenate([prev, cur, nxt], axis=0)
    n_ext = ext.shape[0]
    pos = pos0 + lax.broadcasted_iota(jnp.int32, (tm, 1), 0)
    for grp, win in enumerate(POOL_WINDOWS):
        e = ext[:, grp * gd:(grp + 1) * gd]
        acc = e + pltpu.roll(e, 1, 0)
        half = 1
        while 2 * half < win:
            acc = pltpu.roll(acc, n_ext - half, 0) + pltpu.roll(acc, half, 0)
            half *= 2
        wsum = acc[POOL_HALO:POOL_HALO + tm]
        lo = jnp.maximum(pos - win // 2, 0)
        hi = jnp.minimum(pos + win // 2, seq)
        cnt = (hi - lo).astype(F32)
        diff = wsum / cnt - cur[:, grp * gd:(grp + 1) * gd]
        y = jnp.dot(diff.astype(BF16), w_ref[0, grp], preferred_element_type=F32)
        o_ref[:, grp * gd:(grp + 1) * gd] = (y * s_ref[0, :, grp * gd:(grp + 1) * gd]).astype(o_ref.dtype)


def _pool(proj, w_grp_bf, scale, layer, *, seq, width, col_pool):
    t = proj.shape[0]
    _, ngrp, gd, _ = w_grp_bf.shape
    tm = _tile(seq, 512)
    hb = tm // POOL_HALO
    nhalo = t // POOL_HALO
    cb = col_pool // width
    return pl.pallas_call(
        functools.partial(_pool_kernel, seq=seq, gd=gd),
        grid=(t // tm,),
        in_specs=[pl.BlockSpec((POOL_HALO, width), lambda i: (jnp.maximum(i * hb - 1, 0), cb)),
                  pl.BlockSpec((tm, width), lambda i: (i, cb)),
                  pl.BlockSpec((POOL_HALO, width), lambda i: (jnp.minimum((i + 1) * hb, nhalo - 1), cb)),
                  pl.BlockSpec((1, ngrp, gd, gd), lambda i: (layer, 0, 0, 0)),
                  pl.BlockSpec((1, 1, width), lambda i: (layer, 0, 0))],
        out_specs=pl.BlockSpec((tm, width), lambda i: (i, 0)),
        out_shape=jax.ShapeDtypeStruct((t, width), BF16),
        compiler_params=_cparams("parallel"),
        name="pool",
    )(proj, proj, proj, w_grp_bf, scale)


def _merge_kernel(p_ref, m_ref, wp_ref, wm_ref, gp_ref, gm_ref, o_ref):
    yp = jnp.dot(p_ref[...], wp_ref[0], preferred_element_type=F32)
    ym = jnp.dot(m_ref[...], wm_ref[0], preferred_element_type=F32)
    o_ref[...] = (gp_ref[...].astype(F32) * yp + gm_ref[...].astype(F32) * ym).astype(o_ref.dtype)


def _merge(p, m, wp_bf, wm_bf, layer, proj, col_gp, col_gm):
    t, pw = p.shape
    mw = m.shape[1]
    d = wp_bf.shape[2]
    tm = _tile(t, 1024)
    tn = _tile(d, 1024)
    return pl.pallas_call(
        _merge_kernel,
        grid=(t // tm, d // tn),
        in_specs=[pl.BlockSpec((tm, pw), lambda i, j: (i, 0)),
                  pl.BlockSpec((tm, mw), lambda i, j: (i, 0)),
                  pl.BlockSpec((1, pw, tn), lambda i, j: (layer, 0, j)),
                  pl.BlockSpec((1, mw, tn), lambda i, j: (layer, 0, j)),
                  pl.BlockSpec((tm, tn), lambda i, j: (i, col_gp // tn + j)),
                  pl.BlockSpec((tm, tn), lambda i, j: (i, col_gm // tn + j))],
        out_specs=pl.BlockSpec((tm, tn), lambda i, j: (i, j)),
        out_shape=jax.ShapeDtypeStruct((t, d), BF16),
        compiler_params=_cparams("parallel", "arbitrary"),
        name="merge",
    )(p, m, wp_bf, wm_bf, proj, proj)


def _layer_norm_rows(y, g, b):
    mu = jnp.mean(y, axis=1, keepdims=True)
    cen = y - mu
    var = jnp.mean(cen * cen, axis=1, keepdims=True)
    return cen * lax.rsqrt(var + LN_EPS) * g + b


def _pack_rows(x_bf, xp_ref):
    tm, d = x_bf.shape
    half = d // 2
    nchunks = half // LANES
    bits = lax.bitcast_convert_type(x_bf.astype(F32), jnp.uint32)
    for j in range(nchunks):
        lo = jnp.right_shift(bits[:, j * LANES:(j + 1) * LANES], jnp.uint32(16))
        hi = jnp.bitwise_and(bits[:, half + j * LANES:half + (j + 1) * LANES], jnp.uint32(0xFFFF0000))
        xp_ref[pl.ds(j, tm, stride=nchunks), :] = lax.bitcast_convert_type(lo | hi, jnp.int32)


def _unpack_rows(buf_ref, tm, nchunks):
    lo, hi = [], []
    for j in range(nchunks):
        w = buf_ref[pl.ds(j, tm, stride=nchunks), :]
        lo.append(lax.bitcast_convert_type(jnp.left_shift(w, 16), F32).astype(BF16))
        hi.append(lax.bitcast_convert_type(jnp.bitwise_and(w, jnp.int32(-65536)), F32).astype(BF16))
    return jnp.concatenate(lo + hi, axis=1)


OUTPROJ_SUBTILE = 256


def _outproj_kernel(mix_ref, w_ref, x_ref, g_ref, b_ref, wr_ref, xo_ref, xp_ref, aff_ref, *, alpha, experts):
    tm = mix_ref.shape[0]
    sub = min(tm, OUTPROJ_SUBTILE)
    pack_rows = xp_ref.shape[0] // tm
    for r0 in range(0, tm, sub):
        rows = slice(r0, r0 + sub)
        mix = jnp.dot(mix_ref[rows, :], w_ref[0], preferred_element_type=F32)
        x1 = _layer_norm_rows(alpha * x_ref[rows, :] + mix, g_ref[0], b_ref[0])
        xo_ref[rows, :] = x1
        x1b = x1.astype(BF16)
        _pack_rows(x1b, xp_ref.at[pl.ds(r0 * pack_rows, sub * pack_rows), :])
        logits = jnp.dot(x1b, wr_ref[0], preferred_element_type=F32)
        lane = lax.broadcasted_iota(jnp.int32, logits.shape, 1)
        logits = jnp.where(lane < experts, logits, -jnp.inf)
        e = jnp.exp(logits - jnp.max(logits, axis=1, keepdims=True))
        aff_ref[rows, :] = e / jnp.sum(e, axis=1, keepdims=True)


def _outproj(mixin, wout_bf, x, ln_g, ln_b, wr_bf, layer, *, alpha, experts):
    t, d = x.shape
    tm = _tile(t, 512)
    pack_rows = d // (2 * LANES)
    assert pack_rows == SUBLANES, "one (8,128) int32 tile per packed token row"
    row = lambda i: (i, 0)
    lay = lambda i: (layer, 0, 0)
    return pl.pallas_call(
        functools.partial(_outproj_kernel, alpha=alpha, experts=experts),
        grid=(t // tm,),
        in_specs=[pl.BlockSpec((tm, d), row), pl.BlockSpec((1, d, d), lay), pl.BlockSpec((tm, d), row),
                  pl.BlockSpec((1, 1, d), lay), pl.BlockSpec((1, 1, d), lay), pl.BlockSpec((1, d, LANES), lay)],
        out_specs=[pl.BlockSpec((tm, d), row), pl.BlockSpec((tm * pack_rows, LANES), row),
                   pl.BlockSpec((tm, LANES), row)],
        out_shape=[jax.ShapeDtypeStruct((t, d), F32), jax.ShapeDtypeStruct((t * pack_rows, LANES), jnp.int32),
                   jax.ShapeDtypeStruct((t, LANES), F32)],
        compiler_params=_cparams("parallel"),
        name="outproj_ln",
    )(mixin, wout_bf, x, ln_g, ln_b, wr_bf)


ROUTE_TILE = 512
ROUTE_COUNT_CHUNK = 1024


def _route_kernel(aff_ref, pos_ref, p0_ref, *, cap, tm):
    s = aff_ref.shape[0]
    chunk = min(ROUTE_COUNT_CHUNK, s)

    def count_ge(cand):
        def body(c, acc):
            blk = aff_ref[pl.ds(pl.multiple_of(c * chunk, chunk), chunk), :]
            return acc + jnp.sum(jnp.where(blk >= cand, 1, 0), axis=0, keepdims=True)
        return lax.fori_loop(0, s // chunk, body, jnp.zeros((1, LANES), jnp.int32))

    def bit_body(k, thr_bits):
        cand = thr_bits | jnp.left_shift(jnp.int32(1), 30 - k)
        cnt = count_ge(lax.bitcast_convert_type(cand, F32))
        return jnp.where(cnt >= cap, cand, thr_bits)

    thr_bits = lax.fori_loop(0, 31, bit_body, jnp.zeros((1, LANES), jnp.int32))
    thr = lax.bitcast_convert_type(thr_bits, F32)
    n_gt = count_ge(lax.bitcast_convert_type(thr_bits + 1, F32))
    need = (cap - n_gt).astype(F32)

    r = lax.broadcasted_iota(jnp.int32, (tm, tm), 0)
    c = lax.broadcasted_iota(jnp.int32, (tm, tm), 1)
    tri = jnp.where(c < r, 1.0, 0.0).astype(BF16)

    def tile_body(j, carry):
        run_eq, run_sel = carry
        rows = pl.ds(pl.multiple_of(j * tm, tm), tm)
        blk = aff_ref[rows, :]
        gt = blk > thr
        eq = blk == thr
        eqf = jnp.where(eq, 1.0, 0.0)
        eq_rank = run_eq + jnp.dot(tri, eqf.astype(BF16), preferred_element_type=F32)
        sel = gt | (eq & (eq_rank < need))
        self_ = jnp.where(sel, 1.0, 0.0)
        pos = run_sel + jnp.dot(tri, self_.astype(BF16), preferred_element_type=F32)
        pos_ref[rows, :] = jnp.where(sel, pos, -1.0).astype(jnp.int32)
        p0_ref[0, pl.ds(j, 1), :] = run_sel.astype(jnp.int32)
        return (run_eq + jnp.sum(eqf, axis=0, keepdims=True), run_sel + jnp.sum(self_, axis=0, keepdims=True))

    zero = jnp.zeros((1, LANES), F32)
    lax.fori_loop(0, s // tm, tile_body, (zero, zero))


def _route(aff, *, batch, seq, cap, tm):
    t = aff.shape[0]
    ntiles = seq // tm
    return pl.pallas_call(
        functools.partial(_route_kernel, cap=cap, tm=tm),
        grid=(batch,),
        in_specs=[pl.BlockSpec((seq, LANES), lambda b: (b, 0))],
        out_specs=[pl.BlockSpec((seq, LANES), lambda b: (b, 0)),
                   pl.BlockSpec((1, ntiles, LANES), lambda b: (b, 0, 0))],
        out_shape=[jax.ShapeDtypeStruct((t, LANES), jnp.int32),
                   jax.ShapeDtypeStruct((batch, ntiles, LANES), jnp.int32)],
        compiler_params=_cparams("parallel"),
        name="route",
    )(aff)


SLOT_VALUES = 8
COMPACT_WINDOW = 128


def _compact_kernel(p0_ref, pos_ref, aff_ref, out_ref, *, experts, tm, ntiles, win):
    b = pl.program_id(0)
    j = pl.program_id(1)

    @pl.when(j == 0)
    def _():
        out_ref[...] = jnp.zeros_like(out_ref)

    post = pos_ref[...].astype(F32).T
    hi, mid, lo = _split3(aff_ref[...])
    rr = lax.broadcasted_iota(jnp.int32, (LANES, LANES), 0)
    cc = lax.broadcasted_iota(jnp.int32, (LANES, LANES), 1)
    spread = lambda k: jnp.where((cc == rr * SLOT_VALUES + k) & (rr < experts), 1.0, 0.0).astype(BF16)
    vals = (jnp.dot(hi, spread(2), preferred_element_type=F32)
            + jnp.dot(mid, spread(3), preferred_element_type=F32)
            + jnp.dot(lo, spread(4), preferred_element_type=F32))
    tok = j * tm + lax.broadcasted_iota(jnp.int32, (tm, LANES), 0)
    lane = lax.broadcasted_iota(jnp.int32, (tm, LANES), 1)
    k8 = lane % SLOT_VALUES
    vals = jnp.where(k8 == 0, (tok // LANES).astype(F32), jnp.where(k8 == 1, (tok % LANES).astype(F32), vals))
    vals = vals.astype(BF16)
    lane_expert = lane // SLOT_VALUES
    slot_iota = lax.broadcasted_iota(jnp.int32, (win, 1), 0)
    crowded = []
    for e in range(experts):
        base = (b * (ntiles + 1) + j) * experts + e
        p0 = p0_ref[base]
        p1 = p0_ref[base + experts]
        a0 = (p0 // SUBLANES) * SUBLANES
        nwin = (p1 - a0 + win - 1) // win
        ve = jnp.where(lane_expert == e, vals, jnp.zeros_like(vals))
        prow = post[e:e + 1, :]

        def add_window(w, carry, a0=a0, ve=ve, prow=prow):
            start = pl.multiple_of(a0 + w * win, SUBLANES)
            onehot = jnp.where(prow == (start + slot_iota).astype(F32), 1.0, 0.0).astype(BF16)
            out_ref[0, pl.ds(start, win), :] += jnp.dot(onehot, ve, preferred_element_type=F32)
            return carry

        add_window(0, 0)
        crowded.append((nwin, add_window))

    for nwin, add_window in crowded:
        lax.fori_loop(1, nwin, add_window, 0)


def _compact(p0_flat, pos, aff, *, batch, seq, cap, experts, tm):
    ntiles = seq // tm
    win = COMPACT_WINDOW
    cpad = cap + win
    tok = lambda b, j, p0: (b * ntiles + j, 0)
    return pl.pallas_call(
        functools.partial(_compact_kernel, experts=experts, tm=tm, ntiles=ntiles, win=win),
        grid_spec=pltpu.PrefetchScalarGridSpec(
            num_scalar_prefetch=1,
            grid=(batch, ntiles),
            in_specs=[pl.BlockSpec((tm, LANES), tok), pl.BlockSpec((tm, LANES), tok)],
            out_specs=pl.BlockSpec((1, cpad, LANES), lambda b, j, p0: (b, 0, 0))),
        out_shape=jax.ShapeDtypeStruct((batch, cpad, LANES), F32),
        compiler_params=_cparams("parallel", "arbitrary"),
        name="compact",
    )(p0_flat, pos, aff)


EXPERT_TILE = 512


def _expert_kernel(idx_ref, idx_next_ref, xp_hbm, slots_ref, wg_ref, wu_ref, wd_ref, y_ref, buf_ref, sem,
                   *, tm, pack_rows):
    e = pl.program_id(0)
    step = (e * pl.num_programs(1) + pl.program_id(1)) * pl.num_programs(2) + pl.program_id(2)
    last = pl.num_programs(0) * pl.num_programs(1) * pl.num_programs(2) - 1

    def start_rows(ids_ref, slot):
        for i in range(tm):
            src = xp_hbm.at[pl.ds(pl.multiple_of(ids_ref[0, 0, i] * pack_rows, pack_rows), pack_rows), :]
            pltpu.make_async_copy(src, buf_ref.at[slot, pl.ds(i * pack_rows, pack_rows), :], sem.at[slot]).start()

    def wait_rows(slot):
        pltpu.make_async_copy(xp_hbm.at[pl.ds(0, tm * pack_rows), :], buf_ref.at[slot], sem.at[slot]).wait()

    @pl.when(step == 0)
    def _():
        start_rows(idx_ref, 0)

    cur = step % 2
    start_rows(idx_next_ref, 1 - cur)
    wait_rows(cur)

    xe = _unpack_rows(buf_ref.at[cur], tm, pack_rows)
    g = jnp.dot(xe, wg_ref[0, 0], preferred_element_type=F32)
    u = jnp.dot(xe, wu_ref[0, 0], preferred_element_type=F32)
    hid = (g * _sigmoid(g) * u).astype(BF16)
    slots = slots_ref[0]
    lane = lax.broadcasted_iota(jnp.int32, slots.shape, 1)
    is_gate = (lane // SLOT_VALUES == e) & (lane % SLOT_VALUES >= 2) & (lane % SLOT_VALUES <= 4)
    gate = jnp.sum(jnp.where(is_gate, slots, 0.0), axis=1, keepdims=True)
    y = jnp.dot(hid, wd_ref[0, 0], preferred_element_type=F32) * gate
    y_ref[0, 0] = y.astype(y_ref.dtype)

    @pl.when(step == last)
    def _():
        wait_rows(1 - cur)


def _experts(idx_tiles, xp, slots, wg_bf, wu_bf, wd_bf, layer, *, batch, experts, cap, d, tm):
    f = wg_bf.shape[3]
    nct = cap // tm
    nsteps = experts * batch * nct
    pack_rows = d // (2 * LANES)
    step = lambda ei, bi, ci: (ei * batch + bi) * nct + ci
    wmap = lambda ei, bi, ci: (layer, ei, 0, 0)
    return pl.pallas_call(
        functools.partial(_expert_kernel, tm=tm, pack_rows=pack_rows),
        grid=(experts, batch, nct),
        in_specs=[pl.BlockSpec((1, 1, tm), lambda ei, bi, ci: (step(ei, bi, ci), 0, 0), memory_space=pltpu.SMEM),
                  pl.BlockSpec((1, 1, tm), lambda ei, bi, ci: (jnp.minimum(step(ei, bi, ci) + 1, nsteps - 1), 0, 0),
                               memory_space=pltpu.SMEM),
                  pl.BlockSpec(memory_space=pl.ANY),
                  pl.BlockSpec((1, tm, LANES), lambda ei, bi, ci: (bi, ci, 0)),
                  pl.BlockSpec((1, 1, d, f), wmap), pl.BlockSpec((1, 1, d, f), wmap),
                  pl.BlockSpec((1, 1, f, d), wmap)],
        out_specs=pl.BlockSpec((1, 1, tm, d), lambda ei, bi, ci: (bi, ei, ci, 0)),
        out_shape=jax.ShapeDtypeStruct((batch, experts, cap, d), BF16),
        scratch_shapes=[pltpu.VMEM((2, tm * pack_rows, LANES), jnp.int32), pltpu.SemaphoreType.DMA((2,))],
        compiler_params=_cparams("arbitrary", "arbitrary", "arbitrary"),
        name="experts",
    )(idx_tiles, idx_tiles, xp, slots, wg_bf, wu_bf, wd_bf)


COMBINE_SLAB = 96
POS_SPLIT_BITS = 6


def _combine_kernel(p0_ref, y_hbm, pos_ref, x_ref, g_ref, b_ref, spread_ref, xo_ref, xb_ref,
                    slab_ref, extra_ref, acc_ref, sem, sem_extra, *, alpha, experts, tm, ntiles, cap, slab):
    g = pl.program_id(0)
    nsteps = pl.num_programs(0)

    def first_slot(step, e):
        bb = step // ntiles
        p0 = p0_ref[(bb * (ntiles + 1) + step % ntiles) * experts + e]
        return bb, (p0 // SUBLANES) * SUBLANES

    def first_slab_copy(step, slot, e):
        bb, a0 = first_slot(step, e)
        return pltpu.make_async_copy(
            y_hbm.at[bb, e, pl.ds(pl.multiple_of(jnp.minimum(a0, cap - slab), SUBLANES), slab), :],
            slab_ref.at[slot, pl.ds(e * slab, slab), :], sem.at[slot])

    def fetch(step, slot):
        for e in range(experts):
            first_slab_copy(step, slot, e).start()

    @pl.when(g == 0)
    def _():
        fetch(0, 0)

    @pl.when(g + 1 < nsteps)
    def _():
        fetch(g + 1, (g + 1) % 2)

    cur = g % 2
    for e in range(experts):
        first_slab_copy(g, cur, e).wait()

    q = pos_ref[...] + 1
    q_hi = jnp.right_shift(q, POS_SPLIT_BITS).astype(F32).astype(BF16)
    q_lo = jnp.bitwise_and(q, (1 << POS_SPLIT_BITS) - 1).astype(F32).astype(BF16)
    qx = (jnp.dot(q_hi, spread_ref[...], preferred_element_type=F32) * float(1 << POS_SPLIT_BITS)
          + jnp.dot(q_lo, spread_ref[...], preferred_element_type=F32))
    col = lax.broadcasted_iota(jnp.int32, (1, experts * slab), 1)
    target = col % slab + 1
    for e in range(experts):
        _, a0 = first_slot(g, e)
        target = target + jnp.where(col // slab == e, jnp.minimum(a0, cap - slab), 0)
    onehot = jnp.where(qx == target.astype(F32), 1.0, 0.0).astype(BF16)
    acc_ref[...] = jnp.dot(onehot, slab_ref[cur], preferred_element_type=F32)

    slot_iota = lax.broadcasted_iota(jnp.int32, (1, slab), 1)
    for e in range(experts):
        bb, a0 = first_slot(g, e)
        p1 = p0_ref[(bb * (ntiles + 1) + g % ntiles + 1) * experts + e]
        nslab = (p1 - a0 + slab - 1) // slab
        pcol = pos_ref[:, e:e + 1]

        def more(w, carry, bb=bb, a0=a0, pcol=pcol, e=e):
            nominal = a0 + w * slab
            start = jnp.minimum(nominal, cap - slab)
            cp = pltpu.make_async_copy(y_hbm.at[bb, e, pl.ds(pl.multiple_of(start, SUBLANES), slab), :],
                                       extra_ref, sem_extra.at[0])
            cp.start()
            cp.wait()
            hit = (pcol == start + slot_iota) & (pcol >= nominal)
            acc_ref[...] += jnp.dot(jnp.where(hit, 1.0, 0.0).astype(BF16), extra_ref[...],
                                    preferred_element_type=F32)
            return carry

        lax.fori_loop(1, nslab, more, 0)

    x2 = _layer_norm_rows(alpha * x_ref[...] + acc_ref[...], g_ref[0], b_ref[0])
    xo_ref[...] = x2
    xb_ref[...] = x2.astype(BF16)


def _combine(p0_flat, y, pos, x, ln_g, ln_b, layer, *, alpha, batch, seq, cap, experts, tm):
    t, d = x.shape
    ntiles = seq // tm
    slab = min(COMBINE_SLAB, cap)
    assert cap >> POS_SPLIT_BITS < 256 and experts <= LANES
    lane_expert = jnp.arange(experts * slab, dtype=jnp.int32)[None, :] // slab
    spread = (lane_expert == jnp.arange(LANES, dtype=jnp.int32)[:, None]).astype(BF16)
    row = lambda i, p0: (i, 0)
    lay = lambda i, p0: (layer, 0, 0)
    return pl.pallas_call(
        functools.partial(_combine_kernel, alpha=alpha, experts=experts, tm=tm, ntiles=ntiles, cap=cap, slab=slab),
        grid_spec=pltpu.PrefetchScalarGridSpec(
            num_scalar_prefetch=1,
            grid=(t // tm,),
            in_specs=[pl.BlockSpec(memory_space=pl.ANY),
                      pl.BlockSpec((tm, LANES), row), pl.BlockSpec((tm, d), row),
                      pl.BlockSpec((1, 1, d), lay), pl.BlockSpec((1, 1, d), lay),
                      pl.BlockSpec((LANES, experts * slab), lambda i, p0: (0, 0))],
            out_specs=[pl.BlockSpec((tm, d), row), pl.BlockSpec((tm, d), row)],
            scratch_shapes=[pltpu.VMEM((2, experts * slab, d), BF16),
                            pltpu.VMEM((slab, d), BF16),
                            pltpu.VMEM((tm, d), F32),
                            pltpu.SemaphoreType.DMA((2,)),
                            pltpu.SemaphoreType.DMA((1,))]),
        out_shape=[jax.ShapeDtypeStruct((t, d), F32), jax.ShapeDtypeStruct((t, d), BF16)],
        compiler_params=_cparams("arbitrary"),
        name="combine_ln2",
    )(p0_flat, y, pos, x, ln_g, ln_b, spread)


def kernel(x, w_in, b_if, w_pool_grp, pool_scale, w_pool_up, mlstm_norm_w, w_mlstm_up, w_out, ln1_g, ln1_b,
           w_router, w_gate, w_up, w_down, ln2_g, ln2_b):
    batch, seq, d = x.shape
    depth = w_in.shape[0]
    t = batch * seq
    heads = MLSTM_HEADS
    pw = w_pool_up.shape[1]
    mw = w_mlstm_up.shape[1]
    dh = mw // heads
    experts = w_router.shape[2]
    cap = EC_CAPACITY_FACTOR * seq // experts
    ngates = 4 * heads
    alpha = (2.0 * depth) ** 0.25
    lk = _tile(seq, 256)
    rt = _tile(seq, ROUTE_TILE)
    et = _tile(cap, EXPERT_TILE)
    ref_q = pw
    ref_k = ref_q + mw
    ref_if = pw + 4 * mw
    ref_gp = ref_if + ngates
    col_q, col_k, col_v, col_o = 0, mw, 2 * mw, 3 * mw
    col_gp, col_gm = 0, d

    w_in_bf = _wcast(w_in, ref_q, ref_k, dh ** -0.5)
    w_branch_gates = w_in_bf[:, :, ref_gp:]
    w_gates = jnp.concatenate([w_in[:, :, ref_if:ref_gp], jnp.zeros((depth, d, LANES - ngates), F32)],
                              axis=2).astype(BF16)
    gate_bias = jnp.concatenate([b_if, jnp.zeros((depth, LANES - ngates), F32)], axis=1).reshape(depth, 1, LANES)
    w_router_bf = jnp.concatenate([w_router, jnp.zeros((depth, d, LANES - experts), F32)], axis=2).astype(BF16)
    w_pool_grp_bf = w_pool_grp.astype(BF16)
    w_pool_up_bf = w_pool_up.astype(BF16)
    w_mlstm_up_bf = w_mlstm_up.astype(BF16)
    w_out_bf = w_out.astype(BF16)
    w_gate_bf = w_gate.astype(BF16)
    w_up_bf = w_up.astype(BF16)
    w_down_bf = w_down.astype(BF16)
    row3 = lambda a: a.reshape(depth, 1, a.shape[-1])
    pool_scale3, norm_w3 = row3(pool_scale), row3(mlstm_norm_w)
    ln1_g3, ln1_b3, ln2_g3, ln2_b3 = row3(ln1_g), row3(ln1_b), row3(ln2_g), row3(ln2_b)
    batch_row0 = (jnp.arange(batch, dtype=jnp.int32) * seq)[None, :, None]

    xf = x.reshape(t, d)
    xb = xf.astype(BF16)
    for layer in range(depth):
        proj = _inproj(xb, w_in_bf, layer, col0=ref_q, ncols=4 * mw, sigmoid=False)
        branch_gates = _inproj(xb, w_branch_gates, layer, col0=0, ncols=2 * d, sigmoid=True)
        pool_in = _inproj(xb, w_in_bf, layer, col0=0, ncols=pw, sigmoid=False)
        pc, pr = _gateprep(xb, w_gates, gate_bias, layer, lk, heads)
        common = dict(batch=batch, seq=seq, lk=lk, heads=heads, dh=dh)
        hf = _mlstm(proj, pc, pr, col_q, col_k, col_v, col_o, None, None, layer, reverse=False, **common)
        m = _mlstm(proj, pc, pr, col_q, col_k, col_v, col_o, hf, norm_w3, layer, reverse=True, **common)
        p = _pool(pool_in, w_pool_grp_bf, pool_scale3, layer, seq=seq, width=pw, col_pool=0)
        mixin = _merge(p, m, w_pool_up_bf, w_mlstm_up_bf, layer, branch_gates, col_gp, col_gm)
        x1, x1p, aff = _outproj(mixin, w_out_bf, xf, ln1_g3, ln1_b3, w_router_bf, layer, alpha=alpha, experts=experts)

        pos, p0 = _route(aff, batch=batch, seq=seq, cap=cap, tm=rt)
        p0_flat = jnp.concatenate([p0[:, :, :experts], jnp.full((batch, 1, experts), cap, jnp.int32)],
                                  axis=1).reshape(-1)
        slots = _compact(p0_flat, pos, aff, batch=batch, seq=seq, cap=cap, experts=experts, tm=rt)
        tok = (slots[:, :cap, 0::SLOT_VALUES] * LANES + slots[:, :cap, 1::SLOT_VALUES])[..., :experts]
        idx_tiles = (tok.astype(jnp.int32).transpose(2, 0, 1) + batch_row0).reshape(-1, 1, et)
        y = _experts(idx_tiles, x1p, slots, w_gate_bf, w_up_bf, w_down_bf, layer,
                     batch=batch, experts=experts, cap=cap, d=d, tm=et)
        xf, xb = _combine(p0_flat, y, pos, x1, ln2_g3, ln2_b3, layer,
                          alpha=alpha, batch=batch, seq=seq, cap=cap, experts=experts, tm=rt)
    return xf.reshape(batch, seq, d)
```
